```python
import math
import jax, jax.numpy as jnp
from jax import lax
import numpy as np

D_MODEL = 2048
BATCH = 2
SEQ = 16384
DEPTH = 2
DEC_BATCH = 8
DEC_SEQ = 32
PAST_LEN = 4096

CHUNK = 64
Q_BLOCK = 128
N_HEADS_A = 8
HD_A = 64
W_A = N_HEADS_A * 2 * HD_A
SG_CHUNK = 128
N_GROUPS_B = 4
W_B = 512
GC_B = W_B // N_GROUPS_B
POOL_WINDOWS = (2, 4, 8, 16)
N_GROUPS_C = 4
W_C = 512
GC_C = W_C // N_GROUPS_C
POOL_STATE = 15
N_BRANCH = 3
IN_SIZES = (W_A, W_A, W_A, W_B, W_B, W_C, N_BRANCH * D_MODEL)
IN_COLS = W_A * 3 + W_B * 2 + W_C + N_BRANCH * D_MODEL
N_EXPERT_GROUPS = 4
EXPERTS_PER_GROUP = 8
N_EXPERTS = N_EXPERT_GROUPS * EXPERTS_PER_GROUP
TOP_K_FINE = 2
D_EXPERT = 256
MOE_BLOCK = 256
EPS = 1e-6

kernel_name = "hybrid_stream_diffattn_sgu_pool_hmoe"


def rmsnorm(x, g):
    xf = x.astype(jnp.float32)
    y = xf * lax.rsqrt(jnp.mean(xf * xf, axis=-1, keepdims=True) + EPS) * g.astype(jnp.float32)
    return y.astype(x.dtype)


def diff_attend(q, k, v, lam, mask):
    s = jnp.einsum('bqhme,bkhme->bhmqk', q.astype(jnp.float32), k.astype(jnp.float32)) * (HD_A ** -0.5)
    if mask is not None:
        s = jnp.where(mask, s, -jnp.inf)
    p = jax.nn.softmax(s, axis=-1)
    a = p[:, :, 0] - lam * p[:, :, 1]
    return jnp.einsum('bhqk,bkhd->bqhd', a, v.astype(jnp.float32))


def diff_attn_prompt(q, k, v, lam):
    B, S = q.shape[0], q.shape[1]
    nq = S // Q_BLOCK
    qb = jnp.moveaxis(q.reshape(B, nq, Q_BLOCK, N_HEADS_A, 2, HD_A), 1, 0)
    key_chunk = jnp.arange(S) // CHUNK

    def one(args):
        q_blk, i = args
        q_chunk = (i * Q_BLOCK + jnp.arange(Q_BLOCK)) // CHUNK
        mask = key_chunk[None, :] <= q_chunk[:, None]
        return diff_attend(q_blk, k, v, lam, mask)

    o = lax.map(one, (qb, jnp.arange(nq)))
    return jnp.moveaxis(o, 0, 1).reshape(B, S, N_HEADS_A, 2 * HD_A)


def spatial_gate(u, v, w_s, b_s):
    B, L = v.shape[0], v.shape[1]
    c = min(L, SG_CHUNK)
    n = L // c
    tri = jnp.tril(jnp.ones((c, c), dtype=bool))
    w = jnp.where(tri, w_s[:, :c, :c], 0.0)
    vb = v.reshape(B, n, c, N_GROUPS_B, GC_B)
    z = jnp.einsum('gij,bnjgd->bnigd', w, vb) + jnp.transpose(b_s[:, :c])[None, None, :, :, None]
    return u * z.reshape(B, L, W_B).astype(u.dtype)


def pool_mix(xc, prefix, pos0, w_pool, pool_scale):
    B, L = xc.shape[0], xc.shape[1]
    xcat = jnp.concatenate([prefix, xc], axis=1)
    xp = xcat.astype(jnp.float32)
    cs = jnp.concatenate([jnp.zeros((B, 1, W_C), jnp.float32), jnp.cumsum(xp, axis=1)], axis=1)
    end = cs[:, POOL_STATE + 1:]
    pos = pos0 + jnp.arange(L)
    pooled = []
    for gi, w in enumerate(POOL_WINDOWS):
        lo, hi = gi * GC_C, (gi + 1) * GC_C
        start = cs[:, POOL_STATE + 1 - w: POOL_STATE + 1 - w + L, lo:hi]
        cnt = jnp.minimum(w, pos + 1).astype(jnp.float32)[None, :, None]
        pooled.append((end[..., lo:hi] - start) / cnt)
    diff = jnp.concatenate(pooled, axis=-1) - xp[:, POOL_STATE:]
    h = jnp.einsum('blgc,gcd->blgd', diff.reshape(B, L, N_GROUPS_C, GC_C), w_pool.astype(jnp.float32))
    out = (h.reshape(B, L, W_C) * pool_scale.astype(jnp.float32)).astype(xc.dtype)
    return out, xcat[:, -POOL_STATE:]


def hier_moe(x, w_rg, b_rg, w_re, b_re, w_gate, w_up, w_down):
    B, L, D = x.shape
    T = B * L
    nb = -(-T // MOE_BLOCK)
    t = jnp.pad(x.reshape(T, D), ((0, nb * MOE_BLOCK - T), (0, 0))).reshape(nb, MOE_BLOCK, D)

    def one_block(tb):
        tf = tb.astype(jnp.float32)
        p_grp = jax.nn.softmax(tf @ w_rg.astype(jnp.float32) + b_rg.astype(jnp.float32), axis=-1)
        g_sel = jnp.argmax(p_grp, axis=-1)
        g_p = jnp.max(p_grp, axis=-1)
        le = (tf @ w_re.astype(jnp.float32) + b_re.astype(jnp.float32)).reshape(-1, N_EXPERT_GROUPS, EXPERTS_PER_GROUP)
        le = jnp.take_along_axis(le, g_sel[:, None, None], axis=1)[:, 0]
        top_p, top_i = lax.top_k(jax.nn.softmax(le, axis=-1), TOP_K_FINE)
        top_p = top_p / jnp.sum(top_p, axis=-1, keepdims=True)
        e_idx = g_sel[:, None] * EXPERTS_PER_GROUP + top_i
        combine = jnp.sum(jax.nn.one_hot(e_idx, N_EXPERTS, dtype=jnp.float32) * (g_p[:, None] * top_p)[..., None], axis=1)
        h = jax.nn.silu(jnp.einsum('td,edf->tef', tb, w_gate)) * jnp.einsum('td,edf->tef', tb, w_up)
        return jnp.einsum('tef,efd->td', h * combine[..., None].astype(h.dtype), w_down)

    y = lax.map(one_block, t).reshape(nb * MOE_BLOCK, D)[:T]
    return y.reshape(B, L, D).astype(x.dtype)


def encoder_layer(x, l, p, cache_k, cache_v, pool_prev, is_prompt):
    B, L = x.shape[0], x.shape[1]
    xn = rmsnorm(x, p['norm1'])
    proj = xn @ p['w_in']
    split_at = [int(s) for s in np.cumsum(IN_SIZES)[:-1]]
    q, k, v, u_b, v_b, x_c, gates = jnp.split(proj, split_at, axis=-1)
    q = rmsnorm(q.reshape(B, L, N_HEADS_A, 2, HD_A), p['g_q'])
    k = rmsnorm(k.reshape(B, L, N_HEADS_A, 2, HD_A), p['g_k'])
    v = v.reshape(B, L, N_HEADS_A, 2 * HD_A)
    lam_p = p['lam'].astype(jnp.float32)
    lam_init = 0.8 - 0.6 * math.exp(-0.3 * l)
    lam = jnp.exp(jnp.sum(lam_p[0] * lam_p[1])) - jnp.exp(jnp.sum(lam_p[2] * lam_p[3])) + lam_init
    if is_prompt:
        o = diff_attn_prompt(q, k, v, lam)
    else:
        k_all = jnp.concatenate([cache_k, k], axis=1)
        v_all = jnp.concatenate([cache_v, v], axis=1)
        o = diff_attend(q, k_all, v_all, lam, None)
    o_a = (rmsnorm(o, p['g_sub']) * (1.0 - lam_init)).astype(x.dtype).reshape(B, L, W_A)
    v_b = rmsnorm(v_b, p['g_vb'])
    o_b = spatial_gate(u_b, v_b, p['w_sp'], p['b_sp'])
    if is_prompt:
        prefix, pos0 = jnp.zeros((B, POOL_STATE, W_C), x.dtype), 0
    else:
        prefix, pos0 = pool_prev.astype(x.dtype), PAST_LEN
    o_c, pool_new = pool_mix(x_c, prefix, pos0, p['w_pool'], p['pool_scale'])
    g_a, g_b, g_c = jnp.split(jax.nn.sigmoid(gates.astype(jnp.float32)).astype(x.dtype), N_BRANCH, axis=-1)
    wb = p['w_branch']
    merged = (g_a * (o_a @ wb[:W_A]) + g_b * (o_b @ wb[W_A:W_A + W_B])
              + g_c * (o_c @ wb[W_A + W_B:]))
    h = x + merged @ p['w_out']
    y = h + hier_moe(rmsnorm(h, p['norm2']), p['w_rg'], p['b_rg'], p['w_re'], p['b_re'],
                     p['w_e_gate'], p['w_e_up'], p['w_e_down'])
    return y, k, v, pool_new, v_b


def setup_inputs(seed: int = 0) -> dict:
    key = jax.random.key(seed)
    ks = jax.random.split(key, 32)
    f32 = jnp.float32
    nrm = lambda k, s, sc: jax.random.normal(k, s, f32) * sc
    gain = lambda k, s: 1.0 + 0.01 * jax.random.normal(k, s, f32)
    w_branch = jnp.concatenate([
        nrm(ks[13], (DEPTH, W_A, D_MODEL), W_A ** -0.5),
        nrm(ks[14], (DEPTH, W_B, D_MODEL), W_B ** -0.5),
        nrm(ks[15], (DEPTH, W_C, D_MODEL), W_C ** -0.5)], axis=1)
    return {
        "x_prompt": nrm(ks[0], (BATCH, SEQ, D_MODEL), 1.0),
        "x_sample": nrm(ks[1], (DEC_BATCH, DEC_SEQ, D_MODEL), 1.0),
        "cache_k": nrm(ks[2], (DEPTH, DEC_BATCH, PAST_LEN, N_HEADS_A, 2, HD_A), 1.0),
        "cache_v": nrm(ks[3], (DEPTH, DEC_BATCH, PAST_LEN, N_HEADS_A, 2 * HD_A), 1.0),
        "state_pool": nrm(ks[4], (DEPTH, DEC_BATCH, POOL_STATE, W_C), 1.0),
        "norm1": gain(ks[5], (DEPTH, D_MODEL)),
        "w_in": nrm(ks[6], (DEPTH, D_MODEL, IN_COLS), D_MODEL ** -0.5),
        "g_q": gain(ks[7], (DEPTH, HD_A)),
        "g_k": gain(ks[8], (DEPTH, HD_A)),
        "lam": nrm(ks[9], (DEPTH, 4, HD_A), 0.1),
        "g_sub": gain(ks[10], (DEPTH, 2 * HD_A)),
        "g_vb": gain(ks[11], (DEPTH, W_B)),
        "w_sp": nrm(ks[12], (DEPTH, N_GROUPS_B, SG_CHUNK, SG_CHUNK), SG_CHUNK ** -0.5),
        "b_sp": gain(ks[16], (DEPTH, N_GROUPS_B, SG_CHUNK)),
        "w_pool": nrm(ks[17], (DEPTH, N_GROUPS_C, GC_C, GC_C), GC_C ** -0.5),
        "pool_scale": gain(ks[18], (DEPTH, W_C)),
        "w_branch": w_branch,
        "w_out": nrm(ks[19], (DEPTH, D_MODEL, D_MODEL), D_MODEL ** -0.5),
        "norm2": gain(ks[20], (DEPTH, D_MODEL)),
        "w_rg": nrm(ks[21], (DEPTH, D_MODEL, N_EXPERT_GROUPS), D_MODEL ** -0.5),
        "b_rg": nrm(ks[22], (DEPTH, N_EXPERT_GROUPS), 0.01),
        "w_re": nrm(ks[23], (DEPTH, D_MODEL, N_EXPERTS), D_MODEL ** -0.5),
        "b_re": nrm(ks[24], (DEPTH, N_EXPERTS), 0.01),
        "w_e_gate": nrm(ks[25], (DEPTH, N_EXPERTS, D_MODEL, D_EXPERT), D_MODEL ** -0.5),
        "w_e_up": nrm(ks[26], (DEPTH, N_EXPERTS, D_MODEL, D_EXPERT), D_MODEL ** -0.5),
        "w_e_down": nrm(ks[27], (DEPTH, N_EXPERTS, D_EXPERT, D_MODEL), D_EXPERT ** -0.5),
    }


def reference(x_prompt, x_sample, cache_k, cache_v, state_pool, norm1, w_in, g_q, g_k, lam,
              g_sub, g_vb, w_sp, b_sp, w_pool, pool_scale, w_branch, w_out, norm2,
              w_rg, b_rg, w_re, b_re, w_e_gate, w_e_up, w_e_down):
    xp, xs = x_prompt, x_sample
    kp, vp, pp, ksl, vsl, psl, gsl = [], [], [], [], [], [], []
    for l in range(DEPTH):
        p = dict(norm1=norm1[l], w_in=w_in[l], g_q=g_q[l], g_k=g_k[l], lam=lam[l], g_sub=g_sub[l],
                 g_vb=g_vb[l], w_sp=w_sp[l], b_sp=b_sp[l], w_pool=w_pool[l], pool_scale=pool_scale[l],
                 w_branch=w_branch[l], w_out=w_out[l], norm2=norm2[l], w_rg=w_rg[l], b_rg=b_rg[l],
                 w_re=w_re[l], b_re=b_re[l], w_e_gate=w_e_gate[l], w_e_up=w_e_up[l], w_e_down=w_e_down[l])
        xp, k1, v1, pool1, _ = encoder_layer(xp, l, p, None, None, None, True)
        xs, k2, v2, pool2, vb2 = encoder_layer(xs, l, p, cache_k[l], cache_v[l], state_pool[l], False)
        kp.append(k1); vp.append(v1); pp.append(pool1)
        ksl.append(k2); vsl.append(v2); psl.append(pool2); gsl.append(vb2)
    new_k_prompt = jnp.stack(kp)
    new_v_prompt = jnp.stack(vp)
    new_pool_prompt = jnp.stack(pp)
    new_k_sample = jnp.stack(ksl)
    new_v_sample = jnp.stack(vsl)
    new_pool_sample = jnp.stack(psl)
    new_sgu_v_sample = jnp.stack(gsl)
    return (xp, xs, new_k_prompt, new_v_prompt, new_pool_prompt, new_k_sample, new_v_sample, new_pool_sample, new_sgu_v_sample)
```

```python
import functools
import math

import jax
import jax.numpy as jnp
from jax import lax
from jax.experimental import pallas as pl
from jax.experimental.pallas import tpu as pltpu

F32 = jnp.float32
BF16 = jnp.bfloat16

D_MODEL = 2048
CHUNK = 64
N_HEADS_A = 8
HD_A = 64
W_A = N_HEADS_A * 2 * HD_A
SG_CHUNK = 128
N_GROUPS_B = 4
W_B = 512
POOL_WINDOWS = (2, 4, 8, 16)
N_GROUPS_C = 4
W_C = 512
POOL_STATE = 15
N_BRANCH = 3
IN_COLS = W_A * 3 + W_B * 2 + W_C + N_BRANCH * D_MODEL
N_EXPERT_GROUPS = 4
EXPERTS_PER_GROUP = 8
N_EXPERTS = N_EXPERT_GROUPS * EXPERTS_PER_GROUP
D_EXPERT = 256
EPS = 1e-6

LANES = 128
HALO = 16
NEG = -1e30
VMEM_LIMIT = 58 * 1024 * 1024
ROUTER_OFF = N_EXPERT_GROUPS

IN_TN = 512
_JQ, _JK, _JV, _JBC, _JG = 0, 2, 4, 6, 9
IN_NJ = IN_COLS // IN_TN


def _cparams(sem):
    return pltpu.CompilerParams(dimension_semantics=sem, vmem_limit_bytes=VMEM_LIMIT)


def _inproj_kernel(x_ref, g1_ref, w_ref, e_ref, gq_ref, gk_ref,
                   q_ref, kf_ref, kb_ref, vf_ref, vb_ref, bc_ref, gt_ref, xn_ref):
    j = pl.program_id(1)

    @pl.when(j == 0)
    def _():
        x = x_ref[...]
        ms = jnp.mean(x * x, axis=-1, keepdims=True)
        xn_ref[...] = (x * lax.rsqrt(ms + EPS) * g1_ref[...]).astype(BF16)

    p = jnp.dot(xn_ref[...], w_ref[...], preferred_element_type=F32)

    def headnorm(p, g):
        sq = p * p
        hi = sq.astype(BF16)
        lo = (sq - hi.astype(F32)).astype(BF16)
        ss = (jnp.dot(hi, e_ref[...], preferred_element_type=F32)
              + jnp.dot(lo, e_ref[...], preferred_element_type=F32))
        return p * lax.rsqrt(ss * (1.0 / HD_A) + EPS) * g

    @pl.when(j < _JK)
    def _():
        q_ref[...] = headnorm(p, gq_ref[...]).astype(BF16)

    @pl.when((j >= _JK) & (j < _JV))
    def _():
        kn = headnorm(p, gk_ref[...])
        kf_ref[...] = kn
        kb_ref[...] = kn.astype(BF16)

    @pl.when((j >= _JV) & (j < _JBC))
    def _():
        vf_ref[...] = p
        vb_ref[...] = p.astype(BF16)

    @pl.when((j >= _JBC) & (j < _JG))
    def _():
        bc_ref[...] = p

    @pl.when(j >= _JG)
    def _():
        gt_ref[...] = jax.nn.sigmoid(p).astype(BF16)


def _inproj(x, g1, w_bf, e_mat, gq, gk, tm):
    T = x.shape[0]
    grid = (T // tm, IN_NJ)

    def seg(j0, n):
        return lambda i, j: (i, jnp.clip(j - j0, 0, n - 1))

    out_shape = (
        jax.ShapeDtypeStruct((T, W_A), BF16),
        jax.ShapeDtypeStruct((T, W_A), F32),
        jax.ShapeDtypeStruct((T, W_A), BF16),
        jax.ShapeDtypeStruct((T, W_A), F32),
        jax.ShapeDtypeStruct((T, W_A), BF16),
        jax.ShapeDtypeStruct((T, 2 * W_B + W_C), F32),
        jax.ShapeDtypeStruct((T, N_BRANCH * D_MODEL), BF16),
    )
    blk = lambda f: pl.BlockSpec((tm, IN_TN), f)
    out_specs = (blk(seg(_JQ, 2)), blk(seg(_JK, 2)), blk(seg(_JK, 2)), blk(seg(_JV, 2)),
                 blk(seg(_JV, 2)), blk(seg(_JBC, 3)), blk(seg(_JG, IN_NJ - _JG)))
    in_specs = [
        pl.BlockSpec((tm, D_MODEL), lambda i, j: (i, 0)),
        pl.BlockSpec((1, D_MODEL), lambda i, j: (0, 0)),
        pl.BlockSpec((D_MODEL, IN_TN), lambda i, j: (0, j)),
        pl.BlockSpec((IN_TN, IN_TN), lambda i, j: (0, 0)),
        pl.BlockSpec((1, IN_TN), lambda i, j: (0, 0)),
        pl.BlockSpec((1, IN_TN), lambda i, j: (0, 0)),
    ]
    return pl.pallas_call(
        _inproj_kernel, grid=grid, in_specs=in_specs, out_specs=out_specs, out_shape=out_shape,
        scratch_shapes=[pltpu.VMEM((tm, D_MODEL), BF16)],
        compiler_params=_cparams(("arbitrary", "arbitrary")), name="inproj",
    )(x, g1, w_bf, e_mat, gq, gk)


def _lambda_value(lam_ref, lam_init):
    lp = lam_ref[...]
    a = jnp.sum(lp[0:1] * lp[1:2], axis=-1, keepdims=True)
    b = jnp.sum(lp[2:3] * lp[3:4], axis=-1, keepdims=True)
    return jnp.exp(a) - jnp.exp(b) + lam_init


def _stack_q(q, qs_ref, tq):
    lane = lax.broadcasted_iota(jnp.int32, q.shape, 1)
    zero = jnp.zeros_like(q)
    qs_ref[0:tq, :] = jnp.where(lane < HD_A, q, zero)
    qs_ref[tq:2 * tq, :] = jnp.where(lane >= HD_A, q, zero)


def _attn_finish(acc, l, lam, gsub, tq, lam_init):
    o = acc / l
    o = o[0:tq] - lam * o[tq:2 * tq]
    ms = jnp.mean(o * o, axis=-1, keepdims=True)
    return o * lax.rsqrt(ms + EPS) * gsub * (1.0 - lam_init)


def _attn_prompt_kernel(qi_ref, kj_ref, q_ref, k_ref, v_ref, lam_ref, gsub_ref, o_ref,
                        qs_ref, m_ref, l_ref, acc_ref, *, tq, lam_init):
    t = pl.program_id(2)
    qi = qi_ref[t]
    kj = kj_ref[t]

    @pl.when(kj == 0)
    def _():
        _stack_q(q_ref[...], qs_ref, tq)
        m_ref[...] = jnp.full(m_ref.shape, NEG, F32)
        l_ref[...] = jnp.zeros(l_ref.shape, F32)
        acc_ref[...] = jnp.zeros(acc_ref.shape, F32)

    def step(masked):
        s = lax.dot_general(qs_ref[...], k_ref[...], (((1,), (1,)), ((), ())),
                            preferred_element_type=F32)
        if masked:
            r = lax.broadcasted_iota(jnp.int32, s.shape, 0)
            c = lax.broadcasted_iota(jnp.int32, s.shape, 1)
            s = jnp.where((c // CHUNK) <= ((r % tq) // CHUNK), s, NEG)
        m_prev = m_ref[...]
        m_new = jnp.maximum(m_prev, jnp.max(s, axis=-1, keepdims=True))
        alpha = jnp.exp(m_prev - m_new)
        p = jnp.exp(s - m_new)
        l_ref[...] = alpha * l_ref[...] + jnp.sum(p, axis=-1, keepdims=True)
        acc_ref[...] = alpha * acc_ref[...] + jnp.dot(p.astype(BF16), v_ref[...],
                                                      preferred_element_type=F32)
        m_ref[...] = m_new

    @pl.when(kj < qi)
    def _():
        step(False)

    @pl.when(kj == qi)
    def _():
        step(True)
        lam = _lambda_value(lam_ref, lam_init)
        o_ref[...] = _attn_finish(acc_ref[...], l_ref[...], lam, gsub_ref[...], tq,
                                  lam_init).astype(o_ref.dtype)


def _attn_prompt(q, k, v, lam_p, gsub, B, S, tq, lam_init):
    nq = S // tq
    pairs = [(i, j) for i in range(nq) for j in range(i + 1)]
    qi = jnp.asarray([p[0] for p in pairs], jnp.int32)
    kj = jnp.asarray([p[1] for p in pairs], jnp.int32)
    grid_spec = pltpu.PrefetchScalarGridSpec(
        num_scalar_prefetch=2,
        grid=(B, N_HEADS_A, len(pairs)),
        in_specs=[
            pl.BlockSpec((tq, LANES), lambda b, h, t, qi, kj: (b * nq + qi[t], h)),
            pl.BlockSpec((tq, LANES), lambda b, h, t, qi, kj: (b * nq + kj[t], h)),
            pl.BlockSpec((tq, LANES), lambda b, h, t, qi, kj: (b * nq + kj[t], h)),
            pl.BlockSpec((4, HD_A), lambda b, h, t, qi, kj: (0, 0)),
            pl.BlockSpec((1, LANES), lambda b, h, t, qi, kj: (0, 0)),
        ],
        out_specs=pl.BlockSpec((tq, LANES), lambda b, h, t, qi, kj: (b * nq + qi[t], h)),
        scratch_shapes=[pltpu.VMEM((2 * tq, LANES), BF16), pltpu.VMEM((2 * tq, 1), F32),
                        pltpu.VMEM((2 * tq, 1), F32), pltpu.VMEM((2 * tq, LANES), F32)],
    )
    return pl.pallas_call(
        functools.partial(_attn_prompt_kernel, tq=tq, lam_init=lam_init),
        grid_spec=grid_spec, out_shape=jax.ShapeDtypeStruct((B * S, W_A), BF16),
        compiler_params=_cparams(("arbitrary", "arbitrary", "arbitrary")), name="attn_prompt",
    )(qi, kj, q, k, v, lam_p, gsub)


def _attn_sample_kernel(q_ref, kc_ref, vc_ref, kn_ref, vn_ref, lam_ref, gsub_ref, o_ref,
                        qs_ref, *, tq, lam_init):
    _stack_q(q_ref[...], qs_ref, tq)
    qs = qs_ref[...]
    dn = (((1,), (1,)), ((), ()))
    s_c = lax.dot_general(qs, kc_ref[...].astype(BF16), dn, preferred_element_type=F32)
    s_n = lax.dot_general(qs, kn_ref[...], dn, preferred_element_type=F32)
    m = jnp.maximum(jnp.max(s_c, axis=-1, keepdims=True), jnp.max(s_n, axis=-1, keepdims=True))
    p_c = jnp.exp(s_c - m)
    p_n = jnp.exp(s_n - m)
    l = jnp.sum(p_c, axis=-1, keepdims=True) + jnp.sum(p_n, axis=-1, keepdims=True)
    acc = (jnp.dot(p_c.astype(BF16), vc_ref[...].astype(BF16), preferred_element_type=F32)
           + jnp.dot(p_n.astype(BF16), vn_ref[...], preferred_element_type=F32))
    lam = _lambda_value(lam_ref, lam_init)
    o_ref[...] = _attn_finish(acc, l, lam, gsub_ref[...], tq, lam_init).astype(o_ref.dtype)


def _attn_sample(q, kn, vn, cache_k, cache_v, layer, lam_p, gsub, B, L, lam_init):
    P = cache_k.shape[2]
    cspec = pl.BlockSpec((None, None, P, LANES), lambda b, h: (layer, b, 0, h))
    nspec = pl.BlockSpec((L, LANES), lambda b, h: (b, h))
    return pl.pallas_call(
        functools.partial(_attn_sample_kernel, tq=L, lam_init=lam_init),
        grid=(B, N_HEADS_A),
        in_specs=[nspec, cspec, cspec, nspec, nspec,
                  pl.BlockSpec((4, HD_A), lambda b, h: (0, 0)),
                  pl.BlockSpec((1, LANES), lambda b, h: (0, 0))],
        out_specs=nspec,
        out_shape=jax.ShapeDtypeStruct((B * L, W_A), BF16),
        scratch_shapes=[pltpu.VMEM((2 * L, LANES), BF16)],
        compiler_params=_cparams(("arbitrary", "arbitrary")), name="attn_sample",
    )(q, cache_k, cache_v, kn, vn, lam_p, gsub)


def _branches_kernel(bc_ref, halo_ref, gvb_ref, wsp_ref, bsp_ref, wpool_ref, pscale_ref,
                     ob_ref, oc_ref, vbn_ref, *, tm, csz, seq, pos0):
    i = pl.program_id(0)
    u = bc_ref[:, 0:W_B]
    vb = bc_ref[:, W_B:2 * W_B]
    xc = bc_ref[:, 2 * W_B:2 * W_B + W_C]

    ms = jnp.mean(vb * vb, axis=-1, keepdims=True)
    vbn = vb * lax.rsqrt(ms + EPS) * gvb_ref[...]
    vbn_ref[...] = vbn
    vbn_bf = vbn.astype(BF16)
    for c in range(tm // csz):
        r0 = c * csz
        for g in range(N_GROUPS_B):
            c0 = g * LANES
            z = jnp.dot(wsp_ref[g], vbn_bf[r0:r0 + csz, c0:c0 + LANES],
                        preferred_element_type=F32) + bsp_ref[:, c0:c0 + LANES]
            ob_ref[r0:r0 + csz, c0:c0 + LANES] = (u[r0:r0 + csz, c0:c0 + LANES] * z).astype(BF16)

    ext = jnp.concatenate([halo_ref[0], xc], axis=0)
    row = lax.broadcasted_iota(jnp.int32, (tm, 1), 0)
    pos = (pos0 + (i * tm) % seq + row).astype(F32)
    acc = ext
    shift = 1
    for g, w in enumerate(POOL_WINDOWS):
        while shift < w:
            acc = acc + pltpu.roll(acc, shift, axis=0)
            shift *= 2
        c0 = g * LANES
        cnt = jnp.minimum(float(w), pos + 1.0)
        pooled = acc[HALO:HALO + tm, c0:c0 + LANES] / cnt
        diff = (pooled - xc[:, c0:c0 + LANES]).astype(BF16)
        h = jnp.dot(diff, wpool_ref[g], preferred_element_type=F32)
        oc_ref[:, c0:c0 + LANES] = (h * pscale_ref[:, c0:c0 + LANES]).astype(BF16)


def _branches(bc, halo, gvb, wsp, bsp_full, wpool, pscale, tm, csz, seq, pos0):
    T = bc.shape[0]
    row = lambda i: (i, 0)
    full2 = lambda i: (0, 0)
    full3 = lambda i: (0, 0, 0)
    return pl.pallas_call(
        functools.partial(_branches_kernel, tm=tm, csz=csz, seq=seq, pos0=pos0),
        grid=(T // tm,),
        in_specs=[pl.BlockSpec((tm, 2 * W_B + W_C), row),
                  pl.BlockSpec((1, HALO, W_C), lambda i: (i, 0, 0)),
                  pl.BlockSpec((1, W_B), full2),
                  pl.BlockSpec((N_GROUPS_B, csz, csz), full3),
                  pl.BlockSpec((csz, W_B), full2),
                  pl.BlockSpec((N_GROUPS_C, LANES, LANES), full3),
                  pl.BlockSpec((1, W_C), full2)],
        out_specs=(pl.BlockSpec((tm, W_B), row), pl.BlockSpec((tm, W_C), row),
                   pl.BlockSpec((tm, W_B), row)),
        out_shape=(jax.ShapeDtypeStruct((T, W_B), BF16), jax.ShapeDtypeStruct((T, W_C), BF16),
                   jax.ShapeDtypeStruct((T, W_B), F32)),
        compiler_params=_cparams(("arbitrary",)), name="branches",
    )(bc, halo, gvb, wsp, bsp_full, wpool, pscale)


def _merge_kernel(x_ref, oa_ref, ob_ref, oc_ref, gt_ref, wb_ref, wo_ref, g2_ref,
                  wrh_ref, wrl_ref, br_ref, h_ref, hn_ref, comb_ref):
    ta = jnp.dot(oa_ref[...], wb_ref[0:W_A, :], preferred_element_type=F32)
    tb = jnp.dot(ob_ref[...], wb_ref[W_A:W_A + W_B, :], preferred_element_type=F32)
    tc = jnp.dot(oc_ref[...], wb_ref[W_A + W_B:W_A + W_B + W_C, :], preferred_element_type=F32)
    merged = (gt_ref[:, 0:D_MODEL].astype(F32) * ta
              + gt_ref[:, D_MODEL:2 * D_MODEL].astype(F32) * tb
              + gt_ref[:, 2 * D_MODEL:3 * D_MODEL].astype(F32) * tc)
    h = x_ref[...] + jnp.dot(merged.astype(BF16), wo_ref[...], preferred_element_type=F32)
    h_ref[...] = h
    ms = jnp.mean(h * h, axis=-1, keepdims=True)
    hn = h * lax.rsqrt(ms + EPS) * g2_ref[...]
    hn_ref[...] = hn.astype(BF16)

    hi = hn.astype(BF16)
    lo = (hn - hi.astype(F32)).astype(BF16)
    lg = (jnp.dot(hi, wrh_ref[...], preferred_element_type=F32)
          + jnp.dot(lo, wrh_ref[...], preferred_element_type=F32)
          + jnp.dot(hi, wrl_ref[...], preferred_element_type=F32)) + br_ref[...]
    lane = lax.broadcasted_iota(jnp.int32, lg.shape, 1).astype(F32)
    big = float(LANES)
    gmask = lane < N_EXPERT_GROUPS
    lgg = jnp.where(gmask, lg, NEG)
    mg = jnp.max(lgg, axis=-1, keepdims=True)
    sg = jnp.sum(jnp.where(gmask, jnp.exp(lgg - mg), 0.0), axis=-1, keepdims=True)
    g_p = 1.0 / sg
    g_sel = jnp.min(jnp.where(lgg == mg, lane, big), axis=-1, keepdims=True)
    e0 = ROUTER_OFF + g_sel * EXPERTS_PER_GROUP
    emask = (lane >= e0) & (lane < e0 + EXPERTS_PER_GROUP)
    le = jnp.where(emask, lg, NEG)
    me = jnp.max(le, axis=-1, keepdims=True)
    ee = jnp.where(emask, jnp.exp(le - me), 0.0)
    pe = ee / jnp.sum(ee, axis=-1, keepdims=True)
    pe = jnp.where(emask, pe, -1.0)
    top1 = jnp.max(pe, axis=-1, keepdims=True)
    i1 = jnp.min(jnp.where(pe == top1, lane, big), axis=-1, keepdims=True)
    pe2 = jnp.where(lane == i1, -1.0, pe)
    top2 = jnp.max(pe2, axis=-1, keepdims=True)
    i2 = jnp.min(jnp.where(pe2 == top2, lane, big), axis=-1, keepdims=True)
    den = top1 + top2
    comb_ref[...] = g_p * (jnp.where(lane == i1, top1 / den, 0.0)
                           + jnp.where(lane == i2, top2 / den, 0.0))


def _merge(x, oa, ob, oc, gt, wb, wo, g2, wrh, wrl, br, tm):
    T = x.shape[0]
    row = lambda i: (i, 0)
    full = lambda i: (0, 0)
    const = lambda shape: pl.BlockSpec(shape, full, pipeline_mode=pl.Buffered(1))
    return pl.pallas_call(
        _merge_kernel, grid=(T // tm,),
        in_specs=[pl.BlockSpec((tm, D_MODEL), row), pl.BlockSpec((tm, W_A), row),
                  pl.BlockSpec((tm, W_B), row), pl.BlockSpec((tm, W_C), row),
                  pl.BlockSpec((tm, N_BRANCH * D_MODEL), row),
                  const((D_MODEL, D_MODEL)), const((D_MODEL, D_MODEL)),
                  pl.BlockSpec((1, D_MODEL), full),
                  const((D_MODEL, LANES)), const((D_MODEL, LANES)),
                  pl.BlockSpec((1, LANES), full)],
        out_specs=(pl.BlockSpec((tm, D_MODEL), row), pl.BlockSpec((tm, D_MODEL), row),
                   pl.BlockSpec((tm, LANES), row)),
        out_shape=(jax.ShapeDtypeStruct((T, D_MODEL), F32), jax.ShapeDtypeStruct((T, D_MODEL), BF16),
                   jax.ShapeDtypeStruct((T, LANES), F32)),
        compiler_params=_cparams(("arbitrary",)), name="merge",
    )(x, oa, ob, oc, gt, wb, wo, g2, wrh, wrl, br)


def _moe_kernel(h_ref, hn_ref, comb_ref, wg_ref, wu_ref, wd_ref, y_ref):
    e = pl.program_id(1)

    @pl.when(e == 0)
    def _():
        y_ref[...] = h_ref[...]

    x = hn_ref[...]
    a = jnp.dot(x, wg_ref[...], preferred_element_type=F32)
    b = jnp.dot(x, wu_ref[...], preferred_element_type=F32)
    comb = comb_ref[...]
    lane = lax.broadcasted_iota(jnp.int32, comb.shape, 1)
    c = jnp.sum(jnp.where(lane == e + ROUTER_OFF, comb, 0.0), axis=-1, keepdims=True)
    hh = (a * jax.nn.sigmoid(a)) * b * c
    y_ref[...] += jnp.dot(hh.astype(BF16), wd_ref[...], preferred_element_type=F32)


def _moe(h, hn, comb, wg, wu, wd, layer, tm):
    T = h.shape[0]
    row = lambda i, e: (i, 0)
    return pl.pallas_call(
        _moe_kernel, grid=(T // tm, N_EXPERTS),
        in_specs=[pl.BlockSpec((tm, D_MODEL), row), pl.BlockSpec((tm, D_MODEL), row),
                  pl.BlockSpec((tm, LANES), row),
                  pl.BlockSpec((None, None, D_MODEL, D_EXPERT), lambda i, e: (layer, e, 0, 0)),
                  pl.BlockSpec((None, None, D_MODEL, D_EXPERT), lambda i, e: (layer, e, 0, 0)),
                  pl.BlockSpec((None, None, D_EXPERT, D_MODEL), lambda i, e: (layer, e, 0, 0))],
        out_specs=pl.BlockSpec((tm, D_MODEL), row),
        out_shape=jax.ShapeDtypeStruct((T, D_MODEL), F32),
        compiler_params=_cparams(("arbitrary", "arbitrary")), name="moe",
    )(h, hn, comb, wg, wu, wd)


def _pick(n, cands):
    for c in cands:
        if n % c == 0:
            return c
    raise ValueError(f"no tile for {n}")


def _layer(x, l, P, w, cache_k, cache_v, pool_prev, B, L, is_prompt):
    T = B * L
    lam_init = 0.8 - 0.6 * math.exp(-0.3 * l)
    tm = _pick(T, (1024, 512, 256))
    q, kf, kb, vf, vb, bc, gt = _inproj(x, P["norm1"], w["w_in"], w["e_mat"], P["gq"], P["gk"], tm)

    if is_prompt:
        tq = _pick(L, (1024, 512, 256, 128, 64))
        oa = _attn_prompt(q, kb, vb, P["lam"], P["g_sub"], B, L, tq, lam_init)
    else:
        oa = _attn_sample(q, kb, vb, cache_k, cache_v, l, P["lam"], P["g_sub"], B, L, lam_init)

    xc = bc[:, 2 * W_B:].reshape(B, L, W_C)
    if is_prompt:
        tmb = _pick(L, (512, 256, 128))
        csz = SG_CHUNK
        nt = L // tmb
        tails = xc.reshape(B, nt, tmb, W_C)[:, :, tmb - HALO:, :]
        halo = jnp.concatenate([jnp.zeros((B, 1, HALO, W_C), F32), tails[:, :-1]], axis=1)
        halo = halo.reshape(B * nt, HALO, W_C)
        pos0 = 0
        pool_new = xc[:, L - POOL_STATE:, :]
    else:
        tmb = L
        csz = L
        halo = jnp.concatenate([jnp.zeros((B, HALO - POOL_STATE, W_C), F32), pool_prev], axis=1)
        pos0 = cache_k.shape[2]
        pool_new = jnp.concatenate([pool_prev, xc], axis=1)[:, -POOL_STATE:, :]
    ob, oc, vbn = _branches(bc, halo, P["g_vb"], w["w_sp"][:, :csz, :csz], w["b_sp_full"][:csz],
                            w["w_pool"], P["pool_scale"], tmb, csz, L, pos0)

    tm2 = _pick(T, (256,))
    h, hn, comb = _merge(x, oa, ob, oc, gt, w["w_branch"], w["w_out"], P["norm2"],
                         w["wr_hi"], w["wr_lo"], P["b_r"], tm2)
    tm3 = _pick(T, (1024, 512, 256))
    y = _moe(h, hn, comb, w["w_e_gate"], w["w_e_up"], w["w_e_down"], l, tm3)
    return y, kf, vf, pool_new, vbn


def kernel(x_prompt, x_sample, cache_k, cache_v, state_pool, norm1, w_in, g_q, g_k, lam, g_sub,
           g_vb, w_sp, b_sp, w_pool, pool_scale, w_branch, w_out, norm2, w_rg, b_rg, w_re, b_re,
           w_e_gate, w_e_up, w_e_down):
    depth = w_in.shape[0]
    B, S, D = x_prompt.shape
    Bs, Ls, _ = x_sample.shape
    PL = cache_k.shape[2]
    ck = cache_k.reshape(depth, Bs, PL, W_A)
    cv = cache_v.reshape(depth, Bs, PL, W_A)

    rep = IN_TN // HD_A
    gidx = jnp.arange(IN_TN) // HD_A
    e_mat = (gidx[:, None] == gidx[None, :]).astype(BF16)
    tri = jnp.tril(jnp.ones((SG_CHUNK, SG_CHUNK), bool))
    weg, weu, wed = w_e_gate.astype(BF16), w_e_up.astype(BF16), w_e_down.astype(BF16)

    xp = x_prompt.reshape(B * S, D)
    xs = x_sample.reshape(Bs * Ls, D)
    outs = {k: [] for k in ("kp", "vp", "pp", "ks", "vs", "ps", "gs")}
    for l in range(depth):
        w_r = jnp.zeros((D, LANES), F32)
        w_r = w_r.at[:, :N_EXPERT_GROUPS].set(w_rg[l]).at[:, ROUTER_OFF:ROUTER_OFF + N_EXPERTS].set(w_re[l])
        wr_hi = w_r.astype(BF16)
        wr_lo = (w_r - wr_hi.astype(F32)).astype(BF16)
        b_r = jnp.zeros((1, LANES), F32)
        b_r = b_r.at[0, :N_EXPERT_GROUPS].set(b_rg[l]).at[0, ROUTER_OFF:ROUTER_OFF + N_EXPERTS].set(b_re[l])
        P = dict(norm1=norm1[l][None], norm2=norm2[l][None],
                 gq=(jnp.tile(g_q[l], rep) * (HD_A ** -0.5))[None], gk=jnp.tile(g_k[l], rep)[None],
                 lam=lam[l], g_sub=g_sub[l][None], g_vb=g_vb[l][None],
                 pool_scale=pool_scale[l][None], b_r=b_r)
        w = dict(w_in=w_in[l].astype(BF16), e_mat=e_mat,
                 w_sp=jnp.where(tri, w_sp[l], 0.0).astype(BF16),
                 b_sp_full=jnp.repeat(jnp.transpose(b_sp[l]), LANES, axis=1),
                 w_pool=w_pool[l].astype(BF16), w_branch=w_branch[l].astype(BF16),
                 w_out=w_out[l].astype(BF16), wr_hi=wr_hi, wr_lo=wr_lo,
                 w_e_gate=weg, w_e_up=weu, w_e_down=wed)
        xp, k1, v1, pool1, _ = _layer(xp, l, P, w, None, None, None, B, S, True)
        xs, k2, v2, pool2, vb2 = _layer(xs, l, P, w, ck, cv, state_pool[l], Bs, Ls, False)
        outs["kp"].append(k1.reshape(B, S, N_HEADS_A, 2, HD_A))
        outs["vp"].append(v1.reshape(B, S, N_HEADS_A, 2 * HD_A))
        outs["pp"].append(pool1)
        outs["ks"].append(k2.reshape(Bs, Ls, N_HEADS_A, 2, HD_A))
        outs["vs"].append(v2.reshape(Bs, Ls, N_HEADS_A, 2 * HD_A))
        outs["ps"].append(pool2)
        outs["gs"].append(vb2.reshape(Bs, Ls, W_B))
    st = lambda k: jnp.stack(outs[k])
    return (xp.reshape(B, S, D), xs.reshape(Bs, Ls, D), st("kp"), st("vp"), st("pp"),
            st("ks"), st("vs"), st("ps"), st("gs"))
```

```python
import functools
import math

import jax
import jax.numpy as jnp
from jax import lax
from jax.experimental import pallas as pl
from jax.experimental.pallas import tpu as pltpu

F32 = jnp.float32
BF16 = jnp.bfloat16

D_MODEL = 2048
CHUNK = 64
N_HEADS_A = 8
HD_A = 64
W_A = N_HEADS_A * 2 * HD_A
SG_CHUNK = 128
N_GROUPS_B = 4
W_B = 512
POOL_WINDOWS = (2, 4, 8, 16)
N_GROUPS_C = 4
W_C = 512
POOL_STATE = 15
N_BRANCH = 3
IN_COLS = W_A * 3 + W_B * 2 + W_C + N_BRANCH * D_MODEL
N_EXPERT_GROUPS = 4
EXPERTS_PER_GROUP = 8
N_EXPERTS = N_EXPERT_GROUPS * EXPERTS_PER_GROUP
D_EXPERT = 256
EPS = 1e-6

LANES = 128
HALO = 16
NEG = -1e30
ATTN_RBLK = 512
QK_SCALE = (HD_A ** -0.5) * math.log2(math.e)
VMEM_LIMIT = 58 * 1024 * 1024
ROUTER_OFF = N_EXPERT_GROUPS

IN_TN = 512
_JQ, _JK, _JV, _JBC, _JG = 0, 2, 4, 6, 9
GATE_COL0 = _JG * IN_TN
GATES_TN = 1024
GATES_RBLK = 256


def _cparams(sem):
    return pltpu.CompilerParams(dimension_semantics=sem, vmem_limit_bytes=VMEM_LIMIT)


def _inproj_kernel(x_ref, g1_ref, w_ref, e_ref, gq_ref, gk_ref,
                   q_ref, kf_ref, kb_ref, vf_ref, vb_ref, bc_ref, xn_ref):
    j = pl.program_id(1)

    @pl.when(j == 0)
    def _():
        x = x_ref[...]
        ms = jnp.mean(x * x, axis=-1, keepdims=True)
        xn_ref[...] = (x * lax.rsqrt(ms + EPS) * g1_ref[...]).astype(BF16)

    p = jnp.dot(xn_ref[...], w_ref[...], preferred_element_type=F32)

    def headnorm(p, g):
        sq = p * p
        hi = sq.astype(BF16)
        lo = (sq - hi.astype(F32)).astype(BF16)
        ss = (jnp.dot(hi, e_ref[...], preferred_element_type=F32)
              + jnp.dot(lo, e_ref[...], preferred_element_type=F32))
        return p * lax.rsqrt(ss * (1.0 / HD_A) + EPS) * g

    @pl.when(j < _JK)
    def _():
        q_ref[...] = headnorm(p, gq_ref[...]).astype(BF16)

    @pl.when((j >= _JK) & (j < _JV))
    def _():
        kn = headnorm(p, gk_ref[...])
        kf_ref[...] = kn
        kb_ref[...] = kn.astype(BF16)

    @pl.when((j >= _JV) & (j < _JBC))
    def _():
        vf_ref[...] = p
        vb_ref[...] = p.astype(BF16)

    @pl.when(j >= _JBC)
    def _():
        bc_ref[...] = p


def _gates_kernel(xn_ref, w_ref, gt_ref, *, tm, rblk):
    for r0 in range(0, tm, rblk):
        p = jnp.dot(xn_ref[r0:r0 + rblk, :], w_ref[...], preferred_element_type=F32)
        gt_ref[r0:r0 + rblk, :] = (0.5 * jnp.tanh(0.5 * p) + 0.5).astype(BF16)


def _gates(xn, w_g, tm):
    T = xn.shape[0]
    ncol = w_g.shape[1]
    return pl.pallas_call(
        functools.partial(_gates_kernel, tm=tm, rblk=min(GATES_RBLK, tm)),
        grid=(T // tm, ncol // GATES_TN),
        in_specs=[pl.BlockSpec((tm, D_MODEL), lambda i, j: (i, 0)),
                  pl.BlockSpec((D_MODEL, GATES_TN), lambda i, j: (0, j))],
        out_specs=pl.BlockSpec((tm, GATES_TN), lambda i, j: (i, j)),
        out_shape=jax.ShapeDtypeStruct((T, ncol), BF16),
        compiler_params=_cparams(("arbitrary", "arbitrary")), name="gates",
    )(xn, w_g)


def _inproj(x, g1, w_bf, e_mat, gq, gk, tm):
    T = x.shape[0]
    grid = (T // tm, _JG)

    def seg(j0, n):
        return lambda i, j: (i, jnp.clip(j - j0, 0, n - 1))

    out_shape = (
        jax.ShapeDtypeStruct((T, W_A), BF16),
        jax.ShapeDtypeStruct((T, W_A), F32),
        jax.ShapeDtypeStruct((T, W_A), BF16),
        jax.ShapeDtypeStruct((T, W_A), F32),
        jax.ShapeDtypeStruct((T, W_A), BF16),
        jax.ShapeDtypeStruct((T, 2 * W_B + W_C), F32),
        jax.ShapeDtypeStruct((T, D_MODEL), BF16),
    )
    blk = lambda f: pl.BlockSpec((tm, IN_TN), f)
    out_specs = (blk(seg(_JQ, 2)), blk(seg(_JK, 2)), blk(seg(_JK, 2)), blk(seg(_JV, 2)),
                 blk(seg(_JV, 2)), blk(seg(_JBC, 3)),
                 pl.BlockSpec((tm, D_MODEL), lambda i, j: (i, 0)))
    in_specs = [
        pl.BlockSpec((tm, D_MODEL), lambda i, j: (i, 0)),
        pl.BlockSpec((1, D_MODEL), lambda i, j: (0, 0)),
        pl.BlockSpec((D_MODEL, IN_TN), lambda i, j: (0, j)),
        pl.BlockSpec((IN_TN, IN_TN), lambda i, j: (0, 0)),
        pl.BlockSpec((1, IN_TN), lambda i, j: (0, 0)),
        pl.BlockSpec((1, IN_TN), lambda i, j: (0, 0)),
    ]
    return pl.pallas_call(
        _inproj_kernel, grid=grid, in_specs=in_specs, out_specs=out_specs, out_shape=out_shape,
        compiler_params=_cparams(("arbitrary", "arbitrary")), name="inproj",
    )(x, g1, w_bf, e_mat, gq, gk)


def _lambda_value(lam_ref, lam_init):
    lp = lam_ref[...]
    a = jnp.sum(lp[0:1] * lp[1:2], axis=-1, keepdims=True)
    b = jnp.sum(lp[2:3] * lp[3:4], axis=-1, keepdims=True)
    return jnp.exp(a) - jnp.exp(b) + lam_init


def _stack_q(q, qs_ref, tq):
    lane = lax.broadcasted_iota(jnp.int32, q.shape, 1)
    zero = jnp.zeros_like(q)
    qs_ref[0:tq, :] = jnp.where(lane < HD_A, q, zero)
    qs_ref[tq:2 * tq, :] = jnp.where(lane >= HD_A, q, zero)


def _attn_finish(acc, l, lam, gsub, tq, lam_init):
    o = acc / l
    o = o[0:tq] - lam * o[tq:2 * tq]
    ms = jnp.mean(o * o, axis=-1, keepdims=True)
    return o * lax.rsqrt(ms + EPS) * gsub * (1.0 - lam_init)


def _attn_prompt_kernel(qi_ref, kj_ref, q_ref, k_ref, v_ref, lam_ref, gsub_ref, o_ref,
                        qs_ref, m_ref, l_ref, acc_ref, *, tq, rblk, lam_init):
    t = pl.program_id(2)
    qi = qi_ref[t]
    kj = kj_ref[t]

    @pl.when(kj == 0)
    def _():
        _stack_q(q_ref[...], qs_ref, tq)
        m_ref[...] = jnp.full(m_ref.shape, NEG, F32)
        l_ref[...] = jnp.zeros(l_ref.shape, F32)
        acc_ref[...] = jnp.zeros(acc_ref.shape, F32)

    def step(masked):
        for rb in range(2 * tq // rblk):
            r0 = rb * rblk
            q0 = r0 % tq
            nk = min(tq, q0 + rblk) if masked else tq
            s = lax.dot_general(qs_ref[r0:r0 + rblk, :], k_ref[0:nk, :], (((1,), (1,)), ((), ())),
                                preferred_element_type=F32)
            if masked:
                r = lax.broadcasted_iota(jnp.int32, s.shape, 0) + q0
                c = lax.broadcasted_iota(jnp.int32, s.shape, 1)
                s = jnp.where((c // CHUNK) <= (r // CHUNK), s, NEG)
            m_prev = m_ref[r0:r0 + rblk, :]
            m_new = jnp.maximum(m_prev, jnp.max(s, axis=-1, keepdims=True))
            alpha = jnp.exp2(m_prev - m_new)
            p = jnp.exp2(s - pltpu.repeat(m_new, nk // LANES, axis=1))
            l_ref[r0:r0 + rblk, :] = alpha * l_ref[r0:r0 + rblk, :] + jnp.sum(p, axis=-1, keepdims=True)
            acc_ref[r0:r0 + rblk, :] = alpha * acc_ref[r0:r0 + rblk, :] + jnp.dot(
                p.astype(BF16), v_ref[0:nk, :], preferred_element_type=F32)
            m_ref[r0:r0 + rblk, :] = m_new

    @pl.when(kj < qi)
    def _():
        step(False)

    @pl.when(kj == qi)
    def _():
        step(True)
        lam = _lambda_value(lam_ref, lam_init)
        o_ref[...] = _attn_finish(acc_ref[...], l_ref[...], lam, gsub_ref[...], tq,
                                  lam_init).astype(o_ref.dtype)


def _attn_prompt(q, k, v, lam_p, gsub, B, S, tq, lam_init):
    nq = S // tq
    pairs = [(i, j) for i in range(nq) for j in range(i + 1)]
    qi = jnp.asarray([p[0] for p in pairs], jnp.int32)
    kj = jnp.asarray([p[1] for p in pairs], jnp.int32)
    grid_spec = pltpu.PrefetchScalarGridSpec(
        num_scalar_prefetch=2,
        grid=(B, N_HEADS_A, len(pairs)),
        in_specs=[
            pl.BlockSpec((tq, LANES), lambda b, h, t, qi, kj: (b * nq + qi[t], h)),
            pl.BlockSpec((tq, LANES), lambda b, h, t, qi, kj: (b * nq + kj[t], h)),
            pl.BlockSpec((tq, LANES), lambda b, h, t, qi, kj: (b * nq + kj[t], h)),
            pl.BlockSpec((4, HD_A), lambda b, h, t, qi, kj: (0, 0)),
            pl.BlockSpec((1, LANES), lambda b, h, t, qi, kj: (0, 0)),
        ],
        out_specs=pl.BlockSpec((tq, LANES), lambda b, h, t, qi, kj: (b * nq + qi[t], h)),
        scratch_shapes=[pltpu.VMEM((2 * tq, LANES), BF16), pltpu.VMEM((2 * tq, LANES), F32),
                        pltpu.VMEM((2 * tq, LANES), F32), pltpu.VMEM((2 * tq, LANES), F32)],
    )
    return pl.pallas_call(
        functools.partial(_attn_prompt_kernel, tq=tq, rblk=min(ATTN_RBLK, tq), lam_init=lam_init),
        grid_spec=grid_spec, out_shape=jax.ShapeDtypeStruct((B * S, W_A), BF16),
        compiler_params=_cparams(("arbitrary", "arbitrary", "arbitrary")), name="attn_prompt",
    )(qi, kj, q, k, v, lam_p, gsub)


def _attn_sample_kernel(q_ref, kc_ref, vc_ref, kn_ref, vn_ref, lam_ref, gsub_ref, o_ref,
                        qs_ref, *, tq, lam_init):
    _stack_q(q_ref[...], qs_ref, tq)
    qs = qs_ref[...]
    dn = (((1,), (1,)), ((), ()))
    s_c = lax.dot_general(qs, kc_ref[...].astype(BF16), dn, preferred_element_type=F32)
    s_n = lax.dot_general(qs, kn_ref[...], dn, preferred_element_type=F32)
    m = jnp.maximum(jnp.max(s_c, axis=-1, keepdims=True), jnp.max(s_n, axis=-1, keepdims=True))
    p_c = jnp.exp2(s_c - m)
    p_n = jnp.exp2(s_n - m)
    l = jnp.sum(p_c, axis=-1, keepdims=True) + jnp.sum(p_n, axis=-1, keepdims=True)
    acc = (jnp.dot(p_c.astype(BF16), vc_ref[...].astype(BF16), preferred_element_type=F32)
           + jnp.dot(p_n.astype(BF16), vn_ref[...], preferred_element_type=F32))
    lam = _lambda_value(lam_ref, lam_init)
    o_ref[...] = _attn_finish(acc, l, lam, gsub_ref[...], tq, lam_init).astype(o_ref.dtype)


def _attn_sample(q, kn, vn, cache_k, cache_v, layer, lam_p, gsub, B, L, lam_init):
    P = cache_k.shape[2]
    cspec = pl.BlockSpec((None, None, P, LANES), lambda b, h: (layer, b, 0, h))
    nspec = pl.BlockSpec((L, LANES), lambda b, h: (b, h))
    return pl.pallas_call(
        functools.partial(_attn_sample_kernel, tq=L, lam_init=lam_init),
        grid=(B, N_HEADS_A),
        in_specs=[nspec, cspec, cspec, nspec, nspec,
                  pl.BlockSpec((4, HD_A), lambda b, h: (0, 0)),
                  pl.BlockSpec((1, LANES), lambda b, h: (0, 0))],
        out_specs=nspec,
        out_shape=jax.ShapeDtypeStruct((B * L, W_A), BF16),
        scratch_shapes=[pltpu.VMEM((2 * L, LANES), BF16)],
        compiler_params=_cparams(("arbitrary", "arbitrary")), name="attn_sample",
    )(q, cache_k, cache_v, kn, vn, lam_p, gsub)


def _branches_kernel(bc_ref, halo_ref, gvb_ref, wsp_ref, bsp_ref, wpool_ref, pscale_ref,
                     ob_ref, oc_ref, vbn_ref, *, tm, csz, seq, pos0):
    i = pl.program_id(0)
    u = bc_ref[:, 0:W_B]
    vb = bc_ref[:, W_B:2 * W_B]
    xc = bc_ref[:, 2 * W_B:2 * W_B + W_C]

    ms = jnp.mean(vb * vb, axis=-1, keepdims=True)
    vbn = vb * lax.rsqrt(ms + EPS) * gvb_ref[...]
    vbn_ref[...] = vbn
    vbn_bf = vbn.astype(BF16)
    for c in range(tm // csz):
        r0 = c * csz
        for g in range(N_GROUPS_B):
            c0 = g * LANES
            z = jnp.dot(wsp_ref[g], vbn_bf[r0:r0 + csz, c0:c0 + LANES],
                        preferred_element_type=F32) + bsp_ref[:, c0:c0 + LANES]
            ob_ref[r0:r0 + csz, c0:c0 + LANES] = (u[r0:r0 + csz, c0:c0 + LANES] * z).astype(BF16)

    ext = jnp.concatenate([halo_ref[0], xc], axis=0)
    row = lax.broadcasted_iota(jnp.int32, (tm, 1), 0)
    pos = (pos0 + (i * tm) % seq + row).astype(F32)
    acc = ext
    shift = 1
    for g, w in enumerate(POOL_WINDOWS):
        while shift < w:
            acc = acc + pltpu.roll(acc, shift, axis=0)
            shift *= 2
        c0 = g * LANES
        cnt = jnp.minimum(float(w), pos + 1.0)
        pooled = acc[HALO:HALO + tm, c0:c0 + LANES] / cnt
        diff = (pooled - xc[:, c0:c0 + LANES]).astype(BF16)
        h = jnp.dot(diff, wpool_ref[g], preferred_element_type=F32)
        oc_ref[:, c0:c0 + LANES] = (h * pscale_ref[:, c0:c0 + LANES]).astype(BF16)


def _branches(bc, halo, gvb, wsp, bsp_full, wpool, pscale, tm, csz, seq, pos0):
    T = bc.shape[0]
    row = lambda i: (i, 0)
    full2 = lambda i: (0, 0)
    full3 = lambda i: (0, 0, 0)
    return pl.pallas_call(
        functools.partial(_branches_kernel, tm=tm, csz=csz, seq=seq, pos0=pos0),
        grid=(T // tm,),
        in_specs=[pl.BlockSpec((tm, 2 * W_B + W_C), row),
                  pl.BlockSpec((1, HALO, W_C), lambda i: (i, 0, 0)),
                  pl.BlockSpec((1, W_B), full2),
                  pl.BlockSpec((N_GROUPS_B, csz, csz), full3),
                  pl.BlockSpec((csz, W_B), full2),
                  pl.BlockSpec((N_GROUPS_C, LANES, LANES), full3),
                  pl.BlockSpec((1, W_C), full2)],
        out_specs=(pl.BlockSpec((tm, W_B), row), pl.BlockSpec((tm, W_C), row),
                   pl.BlockSpec((tm, W_B), row)),
        out_shape=(jax.ShapeDtypeStruct((T, W_B), BF16), jax.ShapeDtypeStruct((T, W_C), BF16),
                   jax.ShapeDtypeStruct((T, W_B), F32)),
        compiler_params=_cparams(("arbitrary",)), name="branches",
    )(bc, halo, gvb, wsp, bsp_full, wpool, pscale)


def _merge_kernel(x_ref, oa_ref, ob_ref, oc_ref, gt_ref, wb_ref, wo_ref, g2_ref,
                  wrh_ref, wrl_ref, br_ref, h_ref, hn_ref, comb_ref):
    ta = jnp.dot(oa_ref[...], wb_ref[0:W_A, :], preferred_element_type=F32)
    tb = jnp.dot(ob_ref[...], wb_ref[W_A:W_A + W_B, :], preferred_element_type=F32)
    tc = jnp.dot(oc_ref[...], wb_ref[W_A + W_B:W_A + W_B + W_C, :], preferred_element_type=F32)
    merged = (gt_ref[:, 0:D_MODEL].astype(F32) * ta
              + gt_ref[:, D_MODEL:2 * D_MODEL].astype(F32) * tb
              + gt_ref[:, 2 * D_MODEL:3 * D_MODEL].astype(F32) * tc)
    h = x_ref[...] + jnp.dot(merged.astype(BF16), wo_ref[...], preferred_element_type=F32)
    h_ref[...] = h
    ms = jnp.mean(h * h, axis=-1, keepdims=True)
    hn = h * lax.rsqrt(ms + EPS) * g2_ref[...]
    hn_ref[...] = hn.astype(BF16)

    hi = hn.astype(BF16)
    lo = (hn - hi.astype(F32)).astype(BF16)
    lg = (jnp.dot(hi, wrh_ref[...], preferred_element_type=F32)
          + jnp.dot(lo, wrh_ref[...], preferred_element_type=F32)
          + jnp.dot(hi, wrl_ref[...], preferred_element_type=F32)) + br_ref[...]
    lane = lax.broadcasted_iota(jnp.int32, lg.shape, 1).astype(F32)
    big = float(LANES)
    gmask = lane < N_EXPERT_GROUPS
    lgg = jnp.where(gmask, lg, NEG)
    mg = jnp.max(lgg, axis=-1, keepdims=True)
    sg = jnp.sum(jnp.where(gmask, jnp.exp(lgg - mg), 0.0), axis=-1, keepdims=True)
    g_p = 1.0 / sg
    g_sel = jnp.min(jnp.where(lgg == mg, lane, big), axis=-1, keepdims=True)
    e0 = ROUTER_OFF + g_sel * EXPERTS_PER_GROUP
    emask = (lane >= e0) & (lane < e0 + EXPERTS_PER_GROUP)
    le = jnp.where(emask, lg, NEG)
    me = jnp.max(le, axis=-1, keepdims=True)
    ee = jnp.where(emask, jnp.exp(le - me), 0.0)
    pe = ee / jnp.sum(ee, axis=-1, keepdims=True)
    pe = jnp.where(emask, pe, -1.0)
    top1 = jnp.max(pe, axis=-1, keepdims=True)
    i1 = jnp.min(jnp.where(pe == top1, lane, big), axis=-1, keepdims=True)
    pe2 = jnp.where(lane == i1, -1.0, pe)
    top2 = jnp.max(pe2, axis=-1, keepdims=True)
    i2 = jnp.min(jnp.where(pe2 == top2, lane, big), axis=-1, keepdims=True)
    den = top1 + top2
    comb_ref[...] = g_p * (jnp.where(lane == i1, top1 / den, 0.0)
                           + jnp.where(lane == i2, top2 / den, 0.0))


def _merge(x, oa, ob, oc, gt, wb, wo, g2, wrh, wrl, br, tm):
    T = x.shape[0]
    row = lambda i: (i, 0)
    full = lambda i: (0, 0)
    const = lambda shape: pl.BlockSpec(shape, full, pipeline_mode=pl.Buffered(1))
    return pl.pallas_call(
        _merge_kernel, grid=(T // tm,),
        in_specs=[pl.BlockSpec((tm, D_MODEL), row), pl.BlockSpec((tm, W_A), row),
                  pl.BlockSpec((tm, W_B), row), pl.BlockSpec((tm, W_C), row),
                  pl.BlockSpec((tm, N_BRANCH * D_MODEL), row),
                  const((D_MODEL, D_MODEL)), const((D_MODEL, D_MODEL)),
                  pl.BlockSpec((1, D_MODEL), full),
                  const((D_MODEL, LANES)), const((D_MODEL, LANES)),
                  pl.BlockSpec((1, LANES), full)],
        out_specs=(pl.BlockSpec((tm, D_MODEL), row), pl.BlockSpec((tm, D_MODEL), row),
                   pl.BlockSpec((tm, LANES), row)),
        out_shape=(jax.ShapeDtypeStruct((T, D_MODEL), F32), jax.ShapeDtypeStruct((T, D_MODEL), BF16),
                   jax.ShapeDtypeStruct((T, LANES), F32)),
        compiler_params=_cparams(("arbitrary",)), name="merge",
    )(x, oa, ob, oc, gt, wb, wo, g2, wrh, wrl, br)


def _moe_kernel(h_ref, hn_ref, comb_ref, wg_ref, wu_ref, wd_ref, y_ref):
    e = pl.program_id(1)

    @pl.when(e == 0)
    def _():
        y_ref[...] = h_ref[...]

    x = hn_ref[...]
    a = jnp.dot(x, wg_ref[...], preferred_element_type=F32)
    b = jnp.dot(x, wu_ref[...], preferred_element_type=F32)
    comb = comb_ref[...]
    lane = lax.broadcasted_iota(jnp.int32, comb.shape, 1)
    c = jnp.sum(jnp.where(lane == e + ROUTER_OFF, comb, 0.0), axis=-1, keepdims=True)
    hh = (a * jax.nn.sigmoid(a)) * b * c
    y_ref[...] += jnp.dot(hh.astype(BF16), wd_ref[...], preferred_element_type=F32)


def _moe(h, hn, comb, wg, wu, wd, layer, tm):
    T = h.shape[0]
    row = lambda i, e: (i, 0)
    return pl.pallas_call(
        _moe_kernel, grid=(T // tm, N_EXPERTS),
        in_specs=[pl.BlockSpec((tm, D_MODEL), row), pl.BlockSpec((tm, D_MODEL), row),
                  pl.BlockSpec((tm, LANES), row),
                  pl.BlockSpec((None, None, D_MODEL, D_EXPERT), lambda i, e: (layer, e, 0, 0)),
                  pl.BlockSpec((None, None, D_MODEL, D_EXPERT), lambda i, e: (layer, e, 0, 0)),
                  pl.BlockSpec((None, None, D_EXPERT, D_MODEL), lambda i, e: (layer, e, 0, 0))],
        out_specs=pl.BlockSpec((tm, D_MODEL), row),
        out_shape=jax.ShapeDtypeStruct((T, D_MODEL), F32),
        compiler_params=_cparams(("arbitrary", "arbitrary")), name="moe",
    )(h, hn, comb, wg, wu, wd)


def _pick(n, cands):
    for c in cands:
        if n % c == 0:
            return c
    raise ValueError(f"no tile for {n}")


def _layer(x, l, P, w, cache_k, cache_v, pool_prev, B, L, is_prompt):
    T = B * L
    lam_init = 0.8 - 0.6 * math.exp(-0.3 * l)
    tm = _pick(T, (1024, 512, 256))
    q, kf, kb, vf, vb, bc, xn = _inproj(x, P["norm1"], w["w_in"], w["e_mat"], P["gq"], P["gk"], tm)
    gt = _gates(xn, w["w_gates"], tm)

    if is_prompt:
        tq = _pick(L, (1024, 512, 256, 128, 64))
        oa = _attn_prompt(q, kb, vb, P["lam"], P["g_sub"], B, L, tq, lam_init)
    else:
        oa = _attn_sample(q, kb, vb, cache_k, cache_v, l, P["lam"], P["g_sub"], B, L, lam_init)

    xc = bc[:, 2 * W_B:].reshape(B, L, W_C)
    if is_prompt:
        tmb = _pick(L, (512, 256, 128))
        csz = SG_CHUNK
        nt = L // tmb
        tails = xc.reshape(B, nt, tmb, W_C)[:, :, tmb - HALO:, :]
        halo = jnp.concatenate([jnp.zeros((B, 1, HALO, W_C), F32), tails[:, :-1]], axis=1)
        halo = halo.reshape(B * nt, HALO, W_C)
        pos0 = 0
        pool_new = xc[:, L - POOL_STATE:, :]
    else:
        tmb = L
        csz = L
        halo = jnp.concatenate([jnp.zeros((B, HALO - POOL_STATE, W_C), F32), pool_prev], axis=1)
        pos0 = cache_k.shape[2]
        pool_new = jnp.concatenate([pool_prev, xc], axis=1)[:, -POOL_STATE:, :]
    ob, oc, vbn = _branches(bc, halo, P["g_vb"], w["w_sp"][:, :csz, :csz], w["b_sp_full"][:csz],
                            w["w_pool"], P["pool_scale"], tmb, csz, L, pos0)

    tm2 = _pick(T, (256,))
    h, hn, comb = _merge(x, oa, ob, oc, gt, w["w_branch"], w["w_out"], P["norm2"],
                         w["wr_hi"], w["wr_lo"], P["b_r"], tm2)
    tm3 = _pick(T, (1024, 512, 256))
    y = _moe(h, hn, comb, w["w_e_gate"], w["w_e_up"], w["w_e_down"], l, tm3)
    return y, kf, vf, pool_new, vbn


def kernel(x_prompt, x_sample, cache_k, cache_v, state_pool, norm1, w_in, g_q, g_k, lam, g_sub,
           g_vb, w_sp, b_sp, w_pool, pool_scale, w_branch, w_out, norm2, w_rg, b_rg, w_re, b_re,
           w_e_gate, w_e_up, w_e_down):
    depth = w_in.shape[0]
    B, S, D = x_prompt.shape
    Bs, Ls, _ = x_sample.shape
    PL = cache_k.shape[2]
    ck = cache_k.reshape(depth, Bs, PL, W_A)
    cv = cache_v.reshape(depth, Bs, PL, W_A)

    rep = IN_TN // HD_A
    gidx = jnp.arange(IN_TN) // HD_A
    e_mat = (gidx[:, None] == gidx[None, :]).astype(BF16)
    tri = jnp.tril(jnp.ones((SG_CHUNK, SG_CHUNK), bool))
    weg, weu, wed = w_e_gate.astype(BF16), w_e_up.astype(BF16), w_e_down.astype(BF16)

    xp = x_prompt.reshape(B * S, D)
    xs = x_sample.reshape(Bs * Ls, D)
    outs = {k: [] for k in ("kp", "vp", "pp", "ks", "vs", "ps", "gs")}
    for l in range(depth):
        w_r = jnp.zeros((D, LANES), F32)
        w_r = w_r.at[:, :N_EXPERT_GROUPS].set(w_rg[l]).at[:, ROUTER_OFF:ROUTER_OFF + N_EXPERTS].set(w_re[l])
        wr_hi = w_r.astype(BF16)
        wr_lo = (w_r - wr_hi.astype(F32)).astype(BF16)
        b_r = jnp.zeros((1, LANES), F32)
        b_r = b_r.at[0, :N_EXPERT_GROUPS].set(b_rg[l]).at[0, ROUTER_OFF:ROUTER_OFF + N_EXPERTS].set(b_re[l])
        P = dict(norm1=norm1[l][None], norm2=norm2[l][None],
                 gq=(jnp.tile(g_q[l], rep) * QK_SCALE)[None], gk=jnp.tile(g_k[l], rep)[None],
                 lam=lam[l], g_sub=g_sub[l][None], g_vb=g_vb[l][None],
                 pool_scale=pool_scale[l][None], b_r=b_r)
        w = dict(w_in=w_in[l, :, :GATE_COL0].astype(BF16), w_gates=w_in[l, :, GATE_COL0:].astype(BF16),
                 e_mat=e_mat,
                 w_sp=jnp.where(tri, w_sp[l], 0.0).astype(BF16),
                 b_sp_full=jnp.repeat(jnp.transpose(b_sp[l]), LANES, axis=1),
                 w_pool=w_pool[l].astype(BF16), w_branch=w_branch[l].astype(BF16),
                 w_out=w_out[l].astype(BF16), wr_hi=wr_hi, wr_lo=wr_lo,
                 w_e_gate=weg, w_e_up=weu, w_e_down=wed)
        xp, k1, v1, pool1, _ = _layer(xp, l, P, w, None, None, None, B, S, True)
        xs, k2, v2, pool2, vb2 = _layer(xs, l, P, w, ck, cv, state_pool[l], Bs, Ls, False)
        outs["kp"].append(k1.reshape(B, S, N_HEADS_A, 2, HD_A))
        outs["vp"].append(v1.reshape(B, S, N_HEADS_A, 2 * HD_A))
        outs["pp"].append(pool1)
        outs["ks"].append(k2.reshape(Bs, Ls, N_HEADS_A, 2, HD_A))
        outs["vs"].append(v2.reshape(Bs, Ls, N_HEADS_A, 2 * HD_A))
        outs["ps"].append(pool2)
        outs["gs"].append(vb2.reshape(Bs, Ls, W_B))
    st = lambda k: jnp.stack(outs[k])
    return (xp.reshape(B, S, D), xs.reshape(Bs, Ls, D), st("kp"), st("vp"), st("pp"),
            st("ks"), st("vs"), st("ps"), st("gs"))
```

```python
import functools
import math

import jax
import jax.numpy as jnp
from jax import lax
from jax.experimental import pallas as pl
from jax.experimental.pallas import tpu as pltpu

F32 = jnp.float32
BF16 = jnp.bfloat16

D_MODEL = 2048
CHUNK = 64
N_HEADS_A = 8
HD_A = 64
W_A = N_HEADS_A * 2 * HD_A
SG_CHUNK = 128
N_GROUPS_B = 4
W_B = 512
POOL_WINDOWS = (2, 4, 8, 16)
N_GROUPS_C = 4
W_C = 512
POOL_STATE = 15
N_BRANCH = 3
IN_COLS = W_A * 3 + W_B * 2 + W_C + N_BRANCH * D_MODEL
N_EXPERT_GROUPS = 4
EXPERTS_PER_GROUP = 8
N_EXPERTS = N_EXPERT_GROUPS * EXPERTS_PER_GROUP
D_EXPERT = 256
EPS = 1e-6

LANES = 128
HALO = 16
NEG = -1e30
ATTN_RBLK = 512
QK_SCALE = (HD_A ** -0.5) * math.log2(math.e)
VMEM_LIMIT = 58 * 1024 * 1024
ROUTER_OFF = N_EXPERT_GROUPS

IN_TN = 512
_JQ, _JK, _JV, _JBC, _JG = 0, 2, 4, 6, 9
GATE_COL0 = _JG * IN_TN
GATES_TN = 1024
GATES_RBLK = 256
MERGE_TM = 256
MOE_TM = 256


def _cparams(sem):
    return pltpu.CompilerParams(dimension_semantics=sem, vmem_limit_bytes=VMEM_LIMIT)


def _inproj_kernel(x_ref, g1_ref, w_ref, e_ref, gq_ref, gk_ref,
                   q_ref, kf_ref, kb_ref, vf_ref, vb_ref, bc_ref, xn_ref):
    j = pl.program_id(1)

    @pl.when(j == 0)
    def _():
        x = x_ref[...]
        ms = jnp.mean(x * x, axis=-1, keepdims=True)
        xn_ref[...] = (x * lax.rsqrt(ms + EPS) * g1_ref[...]).astype(BF16)

    p = jnp.dot(xn_ref[...], w_ref[...], preferred_element_type=F32)

    def headnorm(p, g):
        sq = p * p
        hi = sq.astype(BF16)
        lo = (sq - hi.astype(F32)).astype(BF16)
        ss = (jnp.dot(hi, e_ref[...], preferred_element_type=F32)
              + jnp.dot(lo, e_ref[...], preferred_element_type=F32))
        return p * lax.rsqrt(ss * (1.0 / HD_A) + EPS) * g

    @pl.when(j < _JK)
    def _():
        q_ref[...] = headnorm(p, gq_ref[...]).astype(BF16)

    @pl.when((j >= _JK) & (j < _JV))
    def _():
        kn = headnorm(p, gk_ref[...])
        kf_ref[...] = kn
        kb_ref[...] = kn.astype(BF16)

    @pl.when((j >= _JV) & (j < _JBC))
    def _():
        vf_ref[...] = p
        vb_ref[...] = p.astype(BF16)

    @pl.when(j >= _JBC)
    def _():
        bc_ref[...] = p


def _gates_kernel(xn_ref, w_ref, gt_ref, *, tm, rblk):
    for r0 in range(0, tm, rblk):
        p = jnp.dot(xn_ref[r0:r0 + rblk, :], w_ref[...], preferred_element_type=F32)
        gt_ref[r0:r0 + rblk, :] = (0.5 * jnp.tanh(0.5 * p) + 0.5).astype(BF16)


def _gates(xn, w_g, tm):
    T = xn.shape[0]
    ncol = w_g.shape[1]
    return pl.pallas_call(
        functools.partial(_gates_kernel, tm=tm, rblk=min(GATES_RBLK, tm)),
        grid=(T // tm, ncol // GATES_TN),
        in_specs=[pl.BlockSpec((tm, D_MODEL), lambda i, j: (i, 0)),
                  pl.BlockSpec((D_MODEL, GATES_TN), lambda i, j: (0, j))],
        out_specs=pl.BlockSpec((tm, GATES_TN), lambda i, j: (i, j)),
        out_shape=jax.ShapeDtypeStruct((T, ncol), BF16),
        compiler_params=_cparams(("arbitrary", "arbitrary")), name="gates",
    )(xn, w_g)


def _inproj(x, g1, w_bf, e_mat, gq, gk, tm):
    T = x.shape[0]
    grid = (T // tm, _JG)

    def seg(j0, n):
        return lambda i, j: (i, jnp.clip(j - j0, 0, n - 1))

    out_shape = (
        jax.ShapeDtypeStruct((T, W_A), BF16),
        jax.ShapeDtypeStruct((T, W_A), F32),
        jax.ShapeDtypeStruct((T, W_A), BF16),
        jax.ShapeDtypeStruct((T, W_A), F32),
        jax.ShapeDtypeStruct((T, W_A), BF16),
        jax.ShapeDtypeStruct((T, 2 * W_B + W_C), F32),
        jax.ShapeDtypeStruct((T, D_MODEL), BF16),
    )
    blk = lambda f: pl.BlockSpec((tm, IN_TN), f)
    out_specs = (blk(seg(_JQ, 2)), blk(seg(_JK, 2)), blk(seg(_JK, 2)), blk(seg(_JV, 2)),
                 blk(seg(_JV, 2)), blk(seg(_JBC, 3)),
                 pl.BlockSpec((tm, D_MODEL), lambda i, j: (i, 0)))
    in_specs = [
        pl.BlockSpec((tm, D_MODEL), lambda i, j: (i, 0)),
        pl.BlockSpec((1, D_MODEL), lambda i, j: (0, 0)),
        pl.BlockSpec((D_MODEL, IN_TN), lambda i, j: (0, j)),
        pl.BlockSpec((IN_TN, IN_TN), lambda i, j: (0, 0)),
        pl.BlockSpec((1, IN_TN), lambda i, j: (0, 0)),
        pl.BlockSpec((1, IN_TN), lambda i, j: (0, 0)),
    ]
    return pl.pallas_call(
        _inproj_kernel, grid=grid, in_specs=in_specs, out_specs=out_specs, out_shape=out_shape,
        compiler_params=_cparams(("arbitrary", "arbitrary")), name="inproj",
    )(x, g1, w_bf, e_mat, gq, gk)


def _lambda_value(lam_ref, lam_init):
    lp = lam_ref[...]
    a = jnp.sum(lp[0:1] * lp[1:2], axis=-1, keepdims=True)
    b = jnp.sum(lp[2:3] * lp[3:4], axis=-1, keepdims=True)
    return jnp.exp(a) - jnp.exp(b) + lam_init


def _stack_q(q, qs_ref, tq):
    lane = lax.broadcasted_iota(jnp.int32, q.shape, 1)
    zero = jnp.zeros_like(q)
    qs_ref[0:tq, :] = jnp.where(lane < HD_A, q, zero)
    qs_ref[tq:2 * tq, :] = jnp.where(lane >= HD_A, q, zero)


def _attn_finish(acc, l, lam, gsub, tq, lam_init):
    o = acc / l
    o = o[0:tq] - lam * o[tq:2 * tq]
    ms = jnp.mean(o * o, axis=-1, keepdims=True)
    return o * lax.rsqrt(ms + EPS) * gsub * (1.0 - lam_init)


def _attn_prompt_kernel(qi_ref, kj_ref, q_ref, k_ref, v_ref, lam_ref, gsub_ref, o_ref,
                        qs_ref, m_ref, l_ref, acc_ref, *, tq, rblk, lam_init):
    t = pl.program_id(2)
    qi = qi_ref[t]
    kj = kj_ref[t]

    @pl.when(kj == 0)
    def _():
        _stack_q(q_ref[...], qs_ref, tq)
        m_ref[...] = jnp.full(m_ref.shape, NEG, F32)
        l_ref[...] = jnp.zeros(l_ref.shape, F32)
        acc_ref[...] = jnp.zeros(acc_ref.shape, F32)

    def step(masked):
        for rb in range(2 * tq // rblk):
            r0 = rb * rblk
            q0 = r0 % tq
            nk = min(tq, q0 + rblk) if masked else tq
            s = lax.dot_general(qs_ref[r0:r0 + rblk, :], k_ref[0:nk, :], (((1,), (1,)), ((), ())),
                                preferred_element_type=F32)
            if masked:
                r = lax.broadcasted_iota(jnp.int32, s.shape, 0) + q0
                c = lax.broadcasted_iota(jnp.int32, s.shape, 1)
                s = jnp.where((c // CHUNK) <= (r // CHUNK), s, NEG)
            m_prev = m_ref[r0:r0 + rblk, :]
            m_new = jnp.maximum(m_prev, jnp.max(s, axis=-1, keepdims=True))
            alpha = jnp.exp2(m_prev - m_new)
            p = jnp.exp2(s - jnp.tile(m_new, (1, nk // LANES)))
            l_ref[r0:r0 + rblk, :] = alpha * l_ref[r0:r0 + rblk, :] + jnp.sum(p, axis=-1, keepdims=True)
            acc_ref[r0:r0 + rblk, :] = alpha * acc_ref[r0:r0 + rblk, :] + jnp.dot(
                p.astype(BF16), v_ref[0:nk, :], preferred_element_type=F32)
            m_ref[r0:r0 + rblk, :] = m_new

    @pl.when(kj < qi)
    def _():
        step(False)

    @pl.when(kj == qi)
    def _():
        step(True)
        lam = _lambda_value(lam_ref, lam_init)
        o_ref[...] = _attn_finish(acc_ref[...], l_ref[...], lam, gsub_ref[...], tq,
                                  lam_init).astype(o_ref.dtype)


def _attn_prompt(q, k, v, lam_p, gsub, B, S, tq, lam_init):
    nq = S // tq
    pairs = [(i, j) for i in range(nq) for j in range(i + 1)]
    qi = jnp.asarray([p[0] for p in pairs], jnp.int32)
    kj = jnp.asarray([p[1] for p in pairs], jnp.int32)
    grid_spec = pltpu.PrefetchScalarGridSpec(
        num_scalar_prefetch=2,
        grid=(B, N_HEADS_A, len(pairs)),
        in_specs=[
            pl.BlockSpec((tq, LANES), lambda b, h, t, qi, kj: (b * nq + qi[t], h)),
            pl.BlockSpec((tq, LANES), lambda b, h, t, qi, kj: (b * nq + kj[t], h)),
            pl.BlockSpec((tq, LANES), lambda b, h, t, qi, kj: (b * nq + kj[t], h)),
            pl.BlockSpec((4, HD_A), lambda b, h, t, qi, kj: (0, 0)),
            pl.BlockSpec((1, LANES), lambda b, h, t, qi, kj: (0, 0)),
        ],
        out_specs=pl.BlockSpec((tq, LANES), lambda b, h, t, qi, kj: (b * nq + qi[t], h)),
        scratch_shapes=[pltpu.VMEM((2 * tq, LANES), BF16), pltpu.VMEM((2 * tq, LANES), F32),
                        pltpu.VMEM((2 * tq, LANES), F32), pltpu.VMEM((2 * tq, LANES), F32)],
    )
    return pl.pallas_call(
        functools.partial(_attn_prompt_kernel, tq=tq, rblk=min(ATTN_RBLK, tq), lam_init=lam_init),
        grid_spec=grid_spec, out_shape=jax.ShapeDtypeStruct((B * S, W_A), BF16),
        compiler_params=_cparams(("arbitrary", "arbitrary", "arbitrary")), name="attn_prompt",
    )(qi, kj, q, k, v, lam_p, gsub)


def _attn_sample_kernel(q_ref, kc_ref, vc_ref, kn_ref, vn_ref, lam_ref, gsub_ref, o_ref,
                        qs_ref, *, tq, lam_init):
    _stack_q(q_ref[...], qs_ref, tq)
    qs = qs_ref[...]
    dn = (((1,), (1,)), ((), ()))
    s_c = lax.dot_general(qs, kc_ref[...].astype(BF16), dn, preferred_element_type=F32)
    s_n = lax.dot_general(qs, kn_ref[...], dn, preferred_element_type=F32)
    m = jnp.maximum(jnp.max(s_c, axis=-1, keepdims=True), jnp.max(s_n, axis=-1, keepdims=True))
    p_c = jnp.exp2(s_c - m)
    p_n = jnp.exp2(s_n - m)
    l = jnp.sum(p_c, axis=-1, keepdims=True) + jnp.sum(p_n, axis=-1, keepdims=True)
    acc = (jnp.dot(p_c.astype(BF16), vc_ref[...].astype(BF16), preferred_element_type=F32)
           + jnp.dot(p_n.astype(BF16), vn_ref[...], preferred_element_type=F32))
    lam = _lambda_value(lam_ref, lam_init)
    o_ref[...] = _attn_finish(acc, l, lam, gsub_ref[...], tq, lam_init).astype(o_ref.dtype)


def _attn_sample(q, kn, vn, cache_k, cache_v, layer, lam_p, gsub, B, L, lam_init):
    P = cache_k.shape[2]
    cspec = pl.BlockSpec((None, None, P, LANES), lambda b, h: (layer, b, 0, h))
    nspec = pl.BlockSpec((L, LANES), lambda b, h: (b, h))
    return pl.pallas_call(
        functools.partial(_attn_sample_kernel, tq=L, lam_init=lam_init),
        grid=(B, N_HEADS_A),
        in_specs=[nspec, cspec, cspec, nspec, nspec,
                  pl.BlockSpec((4, HD_A), lambda b, h: (0, 0)),
                  pl.BlockSpec((1, LANES), lambda b, h: (0, 0))],
        out_specs=nspec,
        out_shape=jax.ShapeDtypeStruct((B * L, W_A), BF16),
        scratch_shapes=[pltpu.VMEM((2 * L, LANES), BF16)],
        compiler_params=_cparams(("arbitrary", "arbitrary")), name="attn_sample",
    )(q, cache_k, cache_v, kn, vn, lam_p, gsub)


def _branches_kernel(bc_ref, halo_ref, gvb_ref, wsp_ref, bsp_ref, wpool_ref, pscale_ref,
                     ob_ref, oc_ref, vbn_ref, *, tm, csz, seq, pos0):
    i = pl.program_id(0)
    u = bc_ref[:, 0:W_B]
    vb = bc_ref[:, W_B:2 * W_B]
    xc = bc_ref[:, 2 * W_B:2 * W_B + W_C]

    ms = jnp.mean(vb * vb, axis=-1, keepdims=True)
    vbn = vb * lax.rsqrt(ms + EPS) * gvb_ref[...]
    vbn_ref[...] = vbn
    vbn_bf = vbn.astype(BF16)
    for c in range(tm // csz):
        r0 = c * csz
        for g in range(N_GROUPS_B):
            c0 = g * LANES
            z = jnp.dot(wsp_ref[g], vbn_bf[r0:r0 + csz, c0:c0 + LANES],
                        preferred_element_type=F32) + bsp_ref[:, c0:c0 + LANES]
            ob_ref[r0:r0 + csz, c0:c0 + LANES] = (u[r0:r0 + csz, c0:c0 + LANES] * z).astype(BF16)

    ext = jnp.concatenate([halo_ref[0], xc], axis=0)
    row = lax.broadcasted_iota(jnp.int32, (tm, 1), 0)
    pos = (pos0 + (i * tm) % seq + row).astype(F32)
    acc = ext
    shift = 1
    for g, w in enumerate(POOL_WINDOWS):
        while shift < w:
            acc = acc + pltpu.roll(acc, shift, axis=0)
            shift *= 2
        c0 = g * LANES
        cnt = jnp.minimum(float(w), pos + 1.0)
        pooled = acc[HALO:HALO + tm, c0:c0 + LANES] / cnt
        diff = (pooled - xc[:, c0:c0 + LANES]).astype(BF16)
        h = jnp.dot(diff, wpool_ref[g], preferred_element_type=F32)
        oc_ref[:, c0:c0 + LANES] = (h * pscale_ref[:, c0:c0 + LANES]).astype(BF16)


def _branches(bc, halo, gvb, wsp, bsp_full, wpool, pscale, tm, csz, seq, pos0):
    T = bc.shape[0]
    row = lambda i: (i, 0)
    full2 = lambda i: (0, 0)
    full3 = lambda i: (0, 0, 0)
    return pl.pallas_call(
        functools.partial(_branches_kernel, tm=tm, csz=csz, seq=seq, pos0=pos0),
        grid=(T // tm,),
        in_specs=[pl.BlockSpec((tm, 2 * W_B + W_C), row),
                  pl.BlockSpec((1, HALO, W_C), lambda i: (i, 0, 0)),
                  pl.BlockSpec((1, W_B), full2),
                  pl.BlockSpec((N_GROUPS_B, csz, csz), full3),
                  pl.BlockSpec((csz, W_B), full2),
                  pl.BlockSpec((N_GROUPS_C, LANES, LANES), full3),
                  pl.BlockSpec((1, W_C), full2)],
        out_specs=(pl.BlockSpec((tm, W_B), row), pl.BlockSpec((tm, W_C), row),
                   pl.BlockSpec((tm, W_B), row)),
        out_shape=(jax.ShapeDtypeStruct((T, W_B), BF16), jax.ShapeDtypeStruct((T, W_C), BF16),
                   jax.ShapeDtypeStruct((T, W_B), F32)),
        compiler_params=_cparams(("arbitrary",)), name="branches",
    )(bc, halo, gvb, wsp, bsp_full, wpool, pscale)


def _pack_halves(x):
    c = x.shape[1] // 2
    lo = lax.bitcast_convert_type(x[:, :c].astype(BF16).astype(F32), jnp.uint32)
    hi = lax.bitcast_convert_type(x[:, c:].astype(BF16).astype(F32), jnp.uint32)
    return (lo >> 16) | (hi & jnp.uint32(0xFFFF0000))


def _unpack_halves(u):
    lo = lax.bitcast_convert_type(u << 16, F32)
    hi = lax.bitcast_convert_type(u & jnp.uint32(0xFFFF0000), F32)
    return lo, hi


def _merge_kernel(x_ref, oa_ref, ob_ref, oc_ref, gt_ref, wb_ref, wo_ref, g2_ref,
                  wrh_ref, wrl_ref, br_ref, tri_ref, h_ref, hp_ref, route_ref, cnt_ref, carry_ref):
    @pl.when(pl.program_id(0) == 0)
    def _():
        carry_ref[...] = jnp.zeros(carry_ref.shape, F32)

    ta = jnp.dot(oa_ref[...], wb_ref[0:W_A, :], preferred_element_type=F32)
    tb = jnp.dot(ob_ref[...], wb_ref[W_A:W_A + W_B, :], preferred_element_type=F32)
    tc = jnp.dot(oc_ref[...], wb_ref[W_A + W_B:W_A + W_B + W_C, :], preferred_element_type=F32)
    merged = (gt_ref[:, 0:D_MODEL].astype(F32) * ta
              + gt_ref[:, D_MODEL:2 * D_MODEL].astype(F32) * tb
              + gt_ref[:, 2 * D_MODEL:3 * D_MODEL].astype(F32) * tc)
    h = x_ref[...] + jnp.dot(merged.astype(BF16), wo_ref[...], preferred_element_type=F32)
    h_ref[...] = h
    ms = jnp.mean(h * h, axis=-1, keepdims=True)
    hn = h * lax.rsqrt(ms + EPS) * g2_ref[...]
    hp_ref[...] = _pack_halves(hn)

    hi = hn.astype(BF16)
    lo = (hn - hi.astype(F32)).astype(BF16)
    lg = (jnp.dot(hi, wrh_ref[...], preferred_element_type=F32)
          + jnp.dot(lo, wrh_ref[...], preferred_element_type=F32)
          + jnp.dot(hi, wrl_ref[...], preferred_element_type=F32)) + br_ref[...]
    lane = lax.broadcasted_iota(jnp.int32, lg.shape, 1).astype(F32)
    big = float(LANES)
    gmask = lane < N_EXPERT_GROUPS
    lgg = jnp.where(gmask, lg, NEG)
    mg = jnp.max(lgg, axis=-1, keepdims=True)
    sg = jnp.sum(jnp.where(gmask, jnp.exp(lgg - mg), 0.0), axis=-1, keepdims=True)
    g_p = 1.0 / sg
    g_sel = jnp.min(jnp.where(lgg == mg, lane, big), axis=-1, keepdims=True)
    e0 = ROUTER_OFF + g_sel * EXPERTS_PER_GROUP
    emask = (lane >= e0) & (lane < e0 + EXPERTS_PER_GROUP)
    le = jnp.where(emask, lg, NEG)
    me = jnp.max(le, axis=-1, keepdims=True)
    ee = jnp.where(emask, jnp.exp(le - me), 0.0)
    pe = ee / jnp.sum(ee, axis=-1, keepdims=True)
    pe = jnp.where(emask, pe, -1.0)
    top1 = jnp.max(pe, axis=-1, keepdims=True)
    i1 = jnp.min(jnp.where(pe == top1, lane, big), axis=-1, keepdims=True)
    pe2 = jnp.where(lane == i1, -1.0, pe)
    top2 = jnp.max(pe2, axis=-1, keepdims=True)
    i2 = jnp.min(jnp.where(pe2 == top2, lane, big), axis=-1, keepdims=True)
    den = top1 + top2
    w1 = g_p * (top1 / den)
    w2 = g_p * (top2 / den)

    oh1 = lane == i1
    oh2 = lane == i2
    oh = jnp.where(oh1 | oh2, 1.0, 0.0)
    prefix = carry_ref[...] + jnp.dot(tri_ref[...], oh.astype(BF16), preferred_element_type=F32)
    rank1 = jnp.sum(jnp.where(oh1, prefix, 0.0), axis=-1, keepdims=True)
    rank2 = jnp.sum(jnp.where(oh2, prefix, 0.0), axis=-1, keepdims=True)
    carry = carry_ref[...] + jnp.sum(oh, axis=0, keepdims=True)
    carry_ref[...] = carry
    cnt_ref[...] = carry
    li = lax.broadcasted_iota(jnp.int32, lg.shape, 1)
    fields = (i1 - ROUTER_OFF, i2 - ROUTER_OFF, rank1, rank2, w1, w2)
    route = jnp.zeros(lg.shape, F32)
    for k, f in enumerate(fields):
        route = jnp.where(li == k, f, route)
    route_ref[...] = route


def _merge(x, oa, ob, oc, gt, wb, wo, g2, wrh, wrl, br, tri, tm):
    T = x.shape[0]
    row = lambda i: (i, 0)
    full = lambda i: (0, 0)
    const = lambda shape: pl.BlockSpec(shape, full, pipeline_mode=pl.Buffered(1))
    return pl.pallas_call(
        _merge_kernel, grid=(T // tm,),
        in_specs=[pl.BlockSpec((tm, D_MODEL), row), pl.BlockSpec((tm, W_A), row),
                  pl.BlockSpec((tm, W_B), row), pl.BlockSpec((tm, W_C), row),
                  pl.BlockSpec((tm, N_BRANCH * D_MODEL), row),
                  const((D_MODEL, D_MODEL)), const((D_MODEL, D_MODEL)),
                  pl.BlockSpec((1, D_MODEL), full),
                  const((D_MODEL, LANES)), const((D_MODEL, LANES)),
                  pl.BlockSpec((1, LANES), full), pl.BlockSpec((tm, tm), full)],
        out_specs=(pl.BlockSpec((tm, D_MODEL), row), pl.BlockSpec((tm, D_MODEL // 2), row),
                   pl.BlockSpec((tm, LANES), row), pl.BlockSpec((1, LANES), full)),
        out_shape=(jax.ShapeDtypeStruct((T, D_MODEL), F32),
                   jax.ShapeDtypeStruct((T, D_MODEL // 2), jnp.uint32),
                   jax.ShapeDtypeStruct((T, LANES), F32), jax.ShapeDtypeStruct((1, LANES), F32)),
        scratch_shapes=[pltpu.VMEM((1, LANES), F32)],
        compiler_params=_cparams(("arbitrary",)), name="merge",
    )(x, oa, ob, oc, gt, wb, wo, g2, wrh, wrl, br, tri)


def _row_copy(src_ref, src_row, dst_ref, dst_row, sem):
    return pltpu.make_async_copy(src_ref.at[pl.ds(src_row, 1)], dst_ref.at[pl.ds(dst_row, 1)], sem)


def _dispatch_kernel(idx_ref, hp_ref, xs_in_ref, xs_ref, sem, *, tm):
    del xs_in_ref

    def start(r, c):
        _row_copy(hp_ref, r, xs_ref, idx_ref[0, 0, r], sem).start()
        _row_copy(hp_ref, r, xs_ref, idx_ref[0, 0, tm + r], sem).start()
        return c

    def wait(r, c):
        _row_copy(hp_ref, r, xs_ref, idx_ref[0, 0, r], sem).wait()
        _row_copy(hp_ref, r, xs_ref, idx_ref[0, 0, tm + r], sem).wait()
        return c

    lax.fori_loop(0, tm, start, 0)
    lax.fori_loop(0, tm, wait, 0)


def _dispatch(idx, hp, xs0, tm):
    T, C = hp.shape
    return pl.pallas_call(
        functools.partial(_dispatch_kernel, tm=tm), grid=(T // tm,),
        in_specs=[pl.BlockSpec((1, 1, 2 * tm), lambda i: (i, 0, 0), memory_space=pltpu.SMEM),
                  pl.BlockSpec((tm, C), lambda i: (i, 0)),
                  pl.BlockSpec(memory_space=pl.ANY)],
        out_specs=pl.BlockSpec(memory_space=pl.ANY),
        out_shape=jax.ShapeDtypeStruct(xs0.shape, xs0.dtype),
        scratch_shapes=[pltpu.SemaphoreType.DMA(())],
        input_output_aliases={2: 0},
        compiler_params=_cparams(("arbitrary",)), name="moe_dispatch",
    )(idx, hp, xs0)


def _expert_kernel(te_ref, nv_ref, xs_ref, wg_ref, wu_ref, wd_ref, ys_ref):
    del te_ref
    t = pl.program_id(0)

    @pl.when(t < nv_ref[0])
    def _():
        lo, hi = _unpack_halves(xs_ref[...])
        lo = lo.astype(BF16)
        hi = hi.astype(BF16)
        half = D_MODEL // 2
        a = (jnp.dot(lo, wg_ref[0:half, :], preferred_element_type=F32)
             + jnp.dot(hi, wg_ref[half:D_MODEL, :], preferred_element_type=F32))
        b = (jnp.dot(lo, wu_ref[0:half, :], preferred_element_type=F32)
             + jnp.dot(hi, wu_ref[half:D_MODEL, :], preferred_element_type=F32))
        hh = ((a * jax.nn.sigmoid(a)) * b).astype(BF16)
        ys_ref[...] = _pack_halves(jnp.dot(hh, wd_ref[...], preferred_element_type=F32))

    @pl.when(t >= nv_ref[0])
    def _():
        ys_ref[...] = jnp.zeros(ys_ref.shape, ys_ref.dtype)


def _experts(tile_expert, n_valid, xs, wg, wu, wd, layer):
    NP, C = xs.shape
    wspec = lambda shape: pl.BlockSpec((None, None) + shape, lambda t, te, nv: (layer, te[t], 0, 0))
    grid_spec = pltpu.PrefetchScalarGridSpec(
        num_scalar_prefetch=2, grid=(NP // MOE_TM,),
        in_specs=[pl.BlockSpec((MOE_TM, C), lambda t, te, nv: (t, 0)),
                  wspec((D_MODEL, D_EXPERT)), wspec((D_MODEL, D_EXPERT)), wspec((D_EXPERT, D_MODEL))],
        out_specs=pl.BlockSpec((MOE_TM, C), lambda t, te, nv: (t, 0)),
    )
    return pl.pallas_call(
        _expert_kernel, grid_spec=grid_spec, out_shape=jax.ShapeDtypeStruct(xs.shape, xs.dtype),
        compiler_params=_cparams(("arbitrary",)), name="moe_experts",
    )(tile_expert, n_valid, xs, wg, wu, wd)


def _combine_kernel(idx_ref, h_ref, route_ref, ys_ref, y_ref, r1_ref, r2_ref, sem, *, tm):
    def start(r, c):
        _row_copy(ys_ref, idx_ref[0, 0, r], r1_ref, r, sem).start()
        _row_copy(ys_ref, idx_ref[0, 0, tm + r], r2_ref, r, sem).start()
        return c

    def wait(r, c):
        _row_copy(ys_ref, idx_ref[0, 0, r], r1_ref, r, sem).wait()
        _row_copy(ys_ref, idx_ref[0, 0, tm + r], r2_ref, r, sem).wait()
        return c

    lax.fori_loop(0, tm, start, 0)
    lax.fori_loop(0, tm, wait, 0)
    route = route_ref[...]
    w1 = route[:, 4:5]
    w2 = route[:, 5:6]
    a_lo, a_hi = _unpack_halves(r1_ref[...])
    b_lo, b_hi = _unpack_halves(r2_ref[...])
    half = D_MODEL // 2
    y_ref[:, 0:half] = h_ref[:, 0:half] + w1 * a_lo + w2 * b_lo
    y_ref[:, half:D_MODEL] = h_ref[:, half:D_MODEL] + w1 * a_hi + w2 * b_hi


def _combine(idx, h, route, ys, tm):
    T = h.shape[0]
    C = ys.shape[1]
    row = lambda i: (i, 0)
    return pl.pallas_call(
        functools.partial(_combine_kernel, tm=tm), grid=(T // tm,),
        in_specs=[pl.BlockSpec((1, 1, 2 * tm), lambda i: (i, 0, 0), memory_space=pltpu.SMEM),
                  pl.BlockSpec((tm, D_MODEL), row), pl.BlockSpec((tm, LANES), row),
                  pl.BlockSpec(memory_space=pl.ANY)],
        out_specs=pl.BlockSpec((tm, D_MODEL), row),
        out_shape=jax.ShapeDtypeStruct((T, D_MODEL), F32),
        scratch_shapes=[pltpu.VMEM((tm, C), jnp.uint32), pltpu.VMEM((tm, C), jnp.uint32),
                        pltpu.SemaphoreType.DMA(())],
        compiler_params=_cparams(("arbitrary",)), name="moe_combine",
    )(idx, h, route, ys)


def _moe_sparse(h, hp, route, counts, wg, wu, wd, layer, tm):
    T = h.shape[0]
    n_pad = 2 * T + N_EXPERTS * MOE_TM
    cnt = counts[0, ROUTER_OFF:ROUTER_OFF + N_EXPERTS].astype(jnp.int32)
    padded = ((cnt + MOE_TM - 1) // MOE_TM) * MOE_TM
    ends = jnp.cumsum(padded)
    offs = ends - padded
    tile_start = jnp.arange(n_pad // MOE_TM, dtype=jnp.int32) * MOE_TM
    tile_expert = jnp.minimum(jnp.sum(tile_start[:, None] >= ends[None, :], axis=1),
                              N_EXPERTS - 1).astype(jnp.int32)
    n_valid = (ends[-1] // MOE_TM).astype(jnp.int32).reshape(1)
    e1 = route[:, 0].astype(jnp.int32)
    e2 = route[:, 1].astype(jnp.int32)
    d1 = offs[e1] + route[:, 2].astype(jnp.int32)
    d2 = offs[e2] + route[:, 3].astype(jnp.int32)
    idx = jnp.concatenate([d1.reshape(T // tm, 1, tm), d2.reshape(T // tm, 1, tm)], axis=2)
    xs0 = jnp.zeros((n_pad, D_MODEL // 2), jnp.uint32)
    xs = _dispatch(idx, hp, xs0, tm)
    ys = _experts(tile_expert, n_valid, xs, wg, wu, wd, layer)
    return _combine(idx, h, route, ys, tm)


def _pick(n, cands):
    for c in cands:
        if n % c == 0:
            return c
    raise ValueError(f"no tile for {n}")


def _layer(x, l, P, w, cache_k, cache_v, pool_prev, B, L, is_prompt):
    T = B * L
    lam_init = 0.8 - 0.6 * math.exp(-0.3 * l)
    tm = _pick(T, (1024, 512, 256))
    q, kf, kb, vf, vb, bc, xn = _inproj(x, P["norm1"], w["w_in"], w["e_mat"], P["gq"], P["gk"], tm)
    gt = _gates(xn, w["w_gates"], tm)

    if is_prompt:
        tq = _pick(L, (1024, 512, 256, 128, 64))
        oa = _attn_prompt(q, kb, vb, P["lam"], P["g_sub"], B, L, tq, lam_init)
    else:
        oa = _attn_sample(q, kb, vb, cache_k, cache_v, l, P["lam"], P["g_sub"], B, L, lam_init)

    xc = bc[:, 2 * W_B:].reshape(B, L, W_C)
    if is_prompt:
        tmb = _pick(L, (512, 256, 128))
        csz = SG_CHUNK
        nt = L // tmb
        tails = xc.reshape(B, nt, tmb, W_C)[:, :, tmb - HALO:, :]
        halo = jnp.concatenate([jnp.zeros((B, 1, HALO, W_C), F32), tails[:, :-1]], axis=1)
        halo = halo.reshape(B * nt, HALO, W_C)
        pos0 = 0
        pool_new = xc[:, L - POOL_STATE:, :]
    else:
        tmb = L
        csz = L
        halo = jnp.concatenate([jnp.zeros((B, HALO - POOL_STATE, W_C), F32), pool_prev], axis=1)
        pos0 = cache_k.shape[2]
        pool_new = jnp.concatenate([pool_prev, xc], axis=1)[:, -POOL_STATE:, :]
    ob, oc, vbn = _branches(bc, halo, P["g_vb"], w["w_sp"][:, :csz, :csz], w["b_sp_full"][:csz],
                            w["w_pool"], P["pool_scale"], tmb, csz, L, pos0)

    tm2 = _pick(T, (MERGE_TM,))
    h, hp, route, counts = _merge(x, oa, ob, oc, gt, w["w_branch"], w["w_out"], P["norm2"],
                                  w["wr_hi"], w["wr_lo"], P["b_r"], w["tri_rank"], tm2)
    y = _moe_sparse(h, hp, route, counts, w["w_e_gate"], w["w_e_up"], w["w_e_down"], l, tm2)
    return y, kf, vf, pool_new, vbn


def kernel(x_prompt, x_sample, cache_k, cache_v, state_pool, norm1, w_in, g_q, g_k, lam, g_sub,
           g_vb, w_sp, b_sp, w_pool, pool_scale, w_branch, w_out, norm2, w_rg, b_rg, w_re, b_re,
           w_e_gate, w_e_up, w_e_down):
    depth = w_in.shape[0]
    B, S, D = x_prompt.shape
    Bs, Ls, _ = x_sample.shape
    PL = cache_k.shape[2]
    ck = cache_k.reshape(depth, Bs, PL, W_A)
    cv = cache_v.reshape(depth, Bs, PL, W_A)

    rep = IN_TN // HD_A
    gidx = jnp.arange(IN_TN) // HD_A
    e_mat = (gidx[:, None] == gidx[None, :]).astype(BF16)
    tri = jnp.tril(jnp.ones((SG_CHUNK, SG_CHUNK), bool))
    weg, weu, wed = w_e_gate.astype(BF16), w_e_up.astype(BF16), w_e_down.astype(BF16)
    ridx = jnp.arange(MERGE_TM)
    tri_rank = (ridx[None, :] < ridx[:, None]).astype(BF16)

    xp = x_prompt.reshape(B * S, D)
    xs = x_sample.reshape(Bs * Ls, D)
    outs = {k: [] for k in ("kp", "vp", "pp", "ks", "vs", "ps", "gs")}
    for l in range(depth):
        w_r = jnp.zeros((D, LANES), F32)
        w_r = w_r.at[:, :N_EXPERT_GROUPS].set(w_rg[l]).at[:, ROUTER_OFF:ROUTER_OFF + N_EXPERTS].set(w_re[l])
        wr_hi = w_r.astype(BF16)
        wr_lo = (w_r - wr_hi.astype(F32)).astype(BF16)
        b_r = jnp.zeros((1, LANES), F32)
        b_r = b_r.at[0, :N_EXPERT_GROUPS].set(b_rg[l]).at[0, ROUTER_OFF:ROUTER_OFF + N_EXPERTS].set(b_re[l])
        P = dict(norm1=norm1[l][None], norm2=norm2[l][None],
                 gq=(jnp.tile(g_q[l], rep) * QK_SCALE)[None], gk=jnp.tile(g_k[l], rep)[None],
                 lam=lam[l], g_sub=g_sub[l][None], g_vb=g_vb[l][None],
                 pool_scale=pool_scale[l][None], b_r=b_r)
        w = dict(w_in=w_in[l, :, :GATE_COL0].astype(BF16), w_gates=w_in[l, :, GATE_COL0:].astype(BF16),
                 e_mat=e_mat,
                 w_sp=jnp.where(tri, w_sp[l], 0.0).astype(BF16),
                 b_sp_full=jnp.repeat(jnp.transpose(b_sp[l]), LANES, axis=1),
                 w_pool=w_pool[l].astype(BF16), w_branch=w_branch[l].astype(BF16),
                 w_out=w_out[l].astype(BF16), wr_hi=wr_hi, wr_lo=wr_lo,
                 w_e_gate=weg, w_e_up=weu, w_e_down=wed, tri_rank=tri_rank)
        xp, k1, v1, pool1, _ = _layer(xp, l, P, w, None, None, None, B, S, True)
        xs, k2, v2, pool2, vb2 = _layer(xs, l, P, w, ck, cv, state_pool[l], Bs, Ls, False)
        outs["kp"].append(k1.reshape(B, S, N_HEADS_A, 2, HD_A))
        outs["vp"].append(v1.reshape(B, S, N_HEADS_A, 2 * HD_A))
        outs["pp"].append(pool1)
        outs["ks"].append(k2.reshape(Bs, Ls, N_HEADS_A, 2, HD_A))
        outs["vs"].append(v2.reshape(Bs, Ls, N_HEADS_A, 2 * HD_A))
        outs["ps"].append(pool2)
        outs["gs"].append(vb2.reshape(Bs, Ls, W_B))
    st = lambda k: jnp.stack(outs[k])
    return (xp.reshape(B, S, D), xs.reshape(Bs, Ls, D), st("kp"), st("vp"), st("pp"),
            st("ks"), st("vs"), st("ps"), st("gs"))
```

```python
import functools
import math

import jax
import jax.numpy as jnp
from jax import lax
from jax.experimental import pallas as pl
from jax.experimental.pallas import tpu as pltpu

F32 = jnp.float32
BF16 = jnp.bfloat16

D_MODEL = 2048
CHUNK = 64
N_HEADS_A = 8
HD_A = 64
W_A = N_HEADS_A * 2 * HD_A
SG_CHUNK = 128
N_GROUPS_B = 4
W_B = 512
POOL_WINDOWS = (2, 4, 8, 16)
N_GROUPS_C = 4
W_C = 512
POOL_STATE = 15
N_BRANCH = 3
IN_COLS = W_A * 3 + W_B * 2 + W_C + N_BRANCH * D_MODEL
N_EXPERT_GROUPS = 4
EXPERTS_PER_GROUP = 8
N_EXPERTS = N_EXPERT_GROUPS * EXPERTS_PER_GROUP
D_EXPERT = 256
EPS = 1e-6

LANES = 128
HALO = 16
NEG = -1e30
ATTN_RBLK = 512
QK_SCALE = (HD_A ** -0.5) * math.log2(math.e)
MAX_UNSHIFTED_SCORE = 48.0
VMEM_LIMIT = 58 * 1024 * 1024
ROUTER_OFF = N_EXPERT_GROUPS

IN_TN = 512
_JQ, _JK, _JV, _JBC, _JG = 0, 2, 4, 6, 9
GATE_COL0 = _JG * IN_TN
GATES_TN = 1024
GATES_RBLK = 256
MERGE_TM = 256
MOE_TM = 256
ROUTE_ROWS = 8
DMA_UNROLL = 8


def _cparams(sem):
    return pltpu.CompilerParams(dimension_semantics=sem, vmem_limit_bytes=VMEM_LIMIT)


def _inproj_kernel(x_ref, g1_ref, w_ref, e_ref, gq_ref, gk_ref,
                   q_ref, kf_ref, kb_ref, vf_ref, vb_ref, bc_ref, xn_ref):
    j = pl.program_id(1)

    @pl.when(j == 0)
    def _():
        x = x_ref[...]
        ms = jnp.mean(x * x, axis=-1, keepdims=True)
        xn_ref[...] = (x * lax.rsqrt(ms + EPS) * g1_ref[...]).astype(BF16)

    p = jnp.dot(xn_ref[...], w_ref[...], preferred_element_type=F32)

    def headnorm(p, g):
        sq = p * p
        hi = sq.astype(BF16)
        lo = (sq - hi.astype(F32)).astype(BF16)
        ss = (jnp.dot(hi, e_ref[...], preferred_element_type=F32)
              + jnp.dot(lo, e_ref[...], preferred_element_type=F32))
        return p * lax.rsqrt(ss * (1.0 / HD_A) + EPS) * g

    @pl.when(j < _JK)
    def _():
        q_ref[...] = headnorm(p, gq_ref[...]).astype(BF16)

    @pl.when((j >= _JK) & (j < _JV))
    def _():
        kn = headnorm(p, gk_ref[...])
        kf_ref[...] = kn
        kb_ref[...] = kn.astype(BF16)

    @pl.when((j >= _JV) & (j < _JBC))
    def _():
        vf_ref[...] = p
        vb_ref[...] = p.astype(BF16)

    @pl.when(j >= _JBC)
    def _():
        bc_ref[...] = p


def _gates_kernel(xn_ref, w_ref, gt_ref, *, tm, rblk):
    for r0 in range(0, tm, rblk):
        p = jnp.dot(xn_ref[r0:r0 + rblk, :], w_ref[...], preferred_element_type=F32)
        gt_ref[r0:r0 + rblk, :] = (0.5 * jnp.tanh(0.5 * p) + 0.5).astype(BF16)


def _gates(xn, w_g, tm):
    T = xn.shape[0]
    ncol = w_g.shape[1]
    return pl.pallas_call(
        functools.partial(_gates_kernel, tm=tm, rblk=min(GATES_RBLK, tm)),
        grid=(T // tm, ncol // GATES_TN),
        in_specs=[pl.BlockSpec((tm, D_MODEL), lambda i, j: (i, 0)),
                  pl.BlockSpec((D_MODEL, GATES_TN), lambda i, j: (0, j))],
        out_specs=pl.BlockSpec((tm, GATES_TN), lambda i, j: (i, j)),
        out_shape=jax.ShapeDtypeStruct((T, ncol), BF16),
        compiler_params=_cparams(("arbitrary", "arbitrary")), name="gates",
    )(xn, w_g)


def _inproj(x, g1, w_bf, e_mat, gq, gk, tm):
    T = x.shape[0]
    grid = (T // tm, _JG)

    def seg(j0, n):
        return lambda i, j: (i, jnp.clip(j - j0, 0, n - 1))

    out_shape = (
        jax.ShapeDtypeStruct((T, W_A), BF16),
        jax.ShapeDtypeStruct((T, W_A), F32),
        jax.ShapeDtypeStruct((T, W_A), BF16),
        jax.ShapeDtypeStruct((T, W_A), F32),
        jax.ShapeDtypeStruct((T, W_A), BF16),
        jax.ShapeDtypeStruct((T, 2 * W_B + W_C), F32),
        jax.ShapeDtypeStruct((T, D_MODEL), BF16),
    )
    blk = lambda f: pl.BlockSpec((tm, IN_TN), f)
    out_specs = (blk(seg(_JQ, 2)), blk(seg(_JK, 2)), blk(seg(_JK, 2)), blk(seg(_JV, 2)),
                 blk(seg(_JV, 2)), blk(seg(_JBC, 3)),
                 pl.BlockSpec((tm, D_MODEL), lambda i, j: (i, 0)))
    in_specs = [
        pl.BlockSpec((tm, D_MODEL), lambda i, j: (i, 0)),
        pl.BlockSpec((1, D_MODEL), lambda i, j: (0, 0)),
        pl.BlockSpec((D_MODEL, IN_TN), lambda i, j: (0, j)),
        pl.BlockSpec((IN_TN, IN_TN), lambda i, j: (0, 0)),
        pl.BlockSpec((1, IN_TN), lambda i, j: (0, 0)),
        pl.BlockSpec((1, IN_TN), lambda i, j: (0, 0)),
    ]
    return pl.pallas_call(
        _inproj_kernel, grid=grid, in_specs=in_specs, out_specs=out_specs, out_shape=out_shape,
        compiler_params=_cparams(("arbitrary", "arbitrary")), name="inproj",
    )(x, g1, w_bf, e_mat, gq, gk)


def _lambda_value(lam_ref, lam_init):
    lp = lam_ref[...]
    a = jnp.sum(lp[0:1] * lp[1:2], axis=-1, keepdims=True)
    b = jnp.sum(lp[2:3] * lp[3:4], axis=-1, keepdims=True)
    return jnp.exp(a) - jnp.exp(b) + lam_init


def _stack_q(q, qs_ref, tq):
    lane = lax.broadcasted_iota(jnp.int32, q.shape, 1)
    zero = jnp.zeros_like(q)
    qs_ref[0:tq, :] = jnp.where(lane < HD_A, q, zero)
    qs_ref[tq:2 * tq, :] = jnp.where(lane >= HD_A, q, zero)


def _attn_finish(acc, l, lam, gsub, tq, lam_init):
    o = acc / l
    o = o[0:tq] - lam * o[tq:2 * tq]
    ms = jnp.mean(o * o, axis=-1, keepdims=True)
    return o * lax.rsqrt(ms + EPS) * gsub * (1.0 - lam_init)


def _attn_prompt_kernel(qi_ref, kj_ref, fast_ref, q_ref, k_ref, v_ref, lam_ref, gsub_ref, o_ref,
                        qs_ref, m_ref, l_ref, acc_ref, *, tq, rblk, lam_init):
    t = pl.program_id(2)
    qi = qi_ref[t]
    kj = kj_ref[t]
    fast = fast_ref[0] != 0

    @pl.when(kj == 0)
    def _():
        _stack_q(q_ref[...], qs_ref, tq)
        m_ref[...] = jnp.full(m_ref.shape, NEG, F32)
        l_ref[...] = jnp.zeros(l_ref.shape, F32)
        acc_ref[...] = jnp.zeros(acc_ref.shape, F32)

    def step(masked, bounded):
        for rb in range(2 * tq // rblk):
            r0 = rb * rblk
            q0 = r0 % tq
            nk = min(tq, q0 + rblk) if masked else tq
            s = lax.dot_general(qs_ref[r0:r0 + rblk, :], k_ref[0:nk, :], (((1,), (1,)), ((), ())),
                                preferred_element_type=F32)
            if masked:
                r = lax.broadcasted_iota(jnp.int32, s.shape, 0) + q0
                c = lax.broadcasted_iota(jnp.int32, s.shape, 1)
                s = jnp.where((c // CHUNK) <= (r // CHUNK), s, NEG)
            if bounded:
                p = jnp.exp2(s)
                l_ref[r0:r0 + rblk, :] += jnp.sum(p, axis=-1, keepdims=True)
                acc_ref[r0:r0 + rblk, :] += jnp.dot(p.astype(BF16), v_ref[0:nk, :],
                                                    preferred_element_type=F32)
                continue
            m_prev = m_ref[r0:r0 + rblk, :]
            m_new = jnp.maximum(m_prev, jnp.max(s, axis=-1, keepdims=True))
            alpha = jnp.exp2(m_prev - m_new)
            p = jnp.exp2(s - jnp.tile(m_new, (1, nk // LANES)))
            l_ref[r0:r0 + rblk, :] = alpha * l_ref[r0:r0 + rblk, :] + jnp.sum(p, axis=-1, keepdims=True)
            acc_ref[r0:r0 + rblk, :] = alpha * acc_ref[r0:r0 + rblk, :] + jnp.dot(
                p.astype(BF16), v_ref[0:nk, :], preferred_element_type=F32)
            m_ref[r0:r0 + rblk, :] = m_new

    for bounded in (True, False):
        use = fast if bounded else jnp.logical_not(fast)
        pl.when((kj < qi) & use)(functools.partial(step, False, bounded))
        pl.when((kj == qi) & use)(functools.partial(step, True, bounded))

    @pl.when(kj == qi)
    def _():
        lam = _lambda_value(lam_ref, lam_init)
        o_ref[...] = _attn_finish(acc_ref[...], l_ref[...], lam, gsub_ref[...], tq,
                                  lam_init).astype(o_ref.dtype)


def _attn_prompt(q, k, v, lam_p, gsub, fast, B, S, tq, lam_init):
    nq = S // tq
    pairs = [(i, j) for i in range(nq) for j in range(i + 1)]
    qi = jnp.asarray([p[0] for p in pairs], jnp.int32)
    kj = jnp.asarray([p[1] for p in pairs], jnp.int32)
    grid_spec = pltpu.PrefetchScalarGridSpec(
        num_scalar_prefetch=3,
        grid=(B, N_HEADS_A, len(pairs)),
        in_specs=[
            pl.BlockSpec((tq, LANES), lambda b, h, t, qi, kj, f: (b * nq + qi[t], h)),
            pl.BlockSpec((tq, LANES), lambda b, h, t, qi, kj, f: (b * nq + kj[t], h)),
            pl.BlockSpec((tq, LANES), lambda b, h, t, qi, kj, f: (b * nq + kj[t], h)),
            pl.BlockSpec((4, HD_A), lambda b, h, t, qi, kj, f: (0, 0)),
            pl.BlockSpec((1, LANES), lambda b, h, t, qi, kj, f: (0, 0)),
        ],
        out_specs=pl.BlockSpec((tq, LANES), lambda b, h, t, qi, kj, f: (b * nq + qi[t], h)),
        scratch_shapes=[pltpu.VMEM((2 * tq, LANES), BF16), pltpu.VMEM((2 * tq, LANES), F32),
                        pltpu.VMEM((2 * tq, LANES), F32), pltpu.VMEM((2 * tq, LANES), F32)],
    )
    return pl.pallas_call(
        functools.partial(_attn_prompt_kernel, tq=tq, rblk=min(ATTN_RBLK, tq), lam_init=lam_init),
        grid_spec=grid_spec, out_shape=jax.ShapeDtypeStruct((B * S, W_A), BF16),
        compiler_params=_cparams(("arbitrary", "arbitrary", "arbitrary")), name="attn_prompt",
    )(qi, kj, fast, q, k, v, lam_p, gsub)


def _attn_sample_kernel(q_ref, kc_ref, vc_ref, kn_ref, vn_ref, lam_ref, gsub_ref, o_ref,
                        qs_ref, *, tq, lam_init):
    _stack_q(q_ref[...], qs_ref, tq)
    qs = qs_ref[...]
    dn = (((1,), (1,)), ((), ()))
    s_c = lax.dot_general(qs, kc_ref[...].astype(BF16), dn, preferred_element_type=F32)
    s_n = lax.dot_general(qs, kn_ref[...], dn, preferred_element_type=F32)
    m = jnp.maximum(jnp.max(s_c, axis=-1, keepdims=True), jnp.max(s_n, axis=-1, keepdims=True))
    p_c = jnp.exp2(s_c - m)
    p_n = jnp.exp2(s_n - m)
    l = jnp.sum(p_c, axis=-1, keepdims=True) + jnp.sum(p_n, axis=-1, keepdims=True)
    acc = (jnp.dot(p_c.astype(BF16), vc_ref[...].astype(BF16), preferred_element_type=F32)
           + jnp.dot(p_n.astype(BF16), vn_ref[...], preferred_element_type=F32))
    lam = _lambda_value(lam_ref, lam_init)
    o_ref[...] = _attn_finish(acc, l, lam, gsub_ref[...], tq, lam_init).astype(o_ref.dtype)


def _attn_sample(q, kn, vn, cache_k, cache_v, layer, lam_p, gsub, B, L, lam_init):
    P = cache_k.shape[2]
    cspec = pl.BlockSpec((None, None, P, LANES), lambda b, h: (layer, b, 0, h))
    nspec = pl.BlockSpec((L, LANES), lambda b, h: (b, h))
    return pl.pallas_call(
        functools.partial(_attn_sample_kernel, tq=L, lam_init=lam_init),
        grid=(B, N_HEADS_A),
        in_specs=[nspec, cspec, cspec, nspec, nspec,
                  pl.BlockSpec((4, HD_A), lambda b, h: (0, 0)),
                  pl.BlockSpec((1, LANES), lambda b, h: (0, 0))],
        out_specs=nspec,
        out_shape=jax.ShapeDtypeStruct((B * L, W_A), BF16),
        scratch_shapes=[pltpu.VMEM((2 * L, LANES), BF16)],
        compiler_params=_cparams(("arbitrary", "arbitrary")), name="attn_sample",
    )(q, cache_k, cache_v, kn, vn, lam_p, gsub)


def _branches_kernel(bc_ref, halo_ref, gvb_ref, wsp_ref, bsp_ref, wpool_ref, pscale_ref,
                     ob_ref, oc_ref, vbn_ref, *, tm, csz, seq, pos0):
    i = pl.program_id(0)
    u = bc_ref[:, 0:W_B]
    vb = bc_ref[:, W_B:2 * W_B]
    xc = bc_ref[:, 2 * W_B:2 * W_B + W_C]

    ms = jnp.mean(vb * vb, axis=-1, keepdims=True)
    vbn = vb * lax.rsqrt(ms + EPS) * gvb_ref[...]
    vbn_ref[...] = vbn
    vbn_bf = vbn.astype(BF16)
    for c in range(tm // csz):
        r0 = c * csz
        for g in range(N_GROUPS_B):
            c0 = g * LANES
            z = jnp.dot(wsp_ref[g], vbn_bf[r0:r0 + csz, c0:c0 + LANES],
                        preferred_element_type=F32) + bsp_ref[:, c0:c0 + LANES]
            ob_ref[r0:r0 + csz, c0:c0 + LANES] = (u[r0:r0 + csz, c0:c0 + LANES] * z).astype(BF16)

    ext = jnp.concatenate([halo_ref[0], xc], axis=0)
    row = lax.broadcasted_iota(jnp.int32, (tm, 1), 0)
    pos = (pos0 + (i * tm) % seq + row).astype(F32)
    acc = ext
    shift = 1
    for g, w in enumerate(POOL_WINDOWS):
        while shift < w:
            acc = acc + pltpu.roll(acc, shift, axis=0)
            shift *= 2
        c0 = g * LANES
        cnt = jnp.minimum(float(w), pos + 1.0)
        pooled = acc[HALO:HALO + tm, c0:c0 + LANES] / cnt
        diff = (pooled - xc[:, c0:c0 + LANES]).astype(BF16)
        h = jnp.dot(diff, wpool_ref[g], preferred_element_type=F32)
        oc_ref[:, c0:c0 + LANES] = (h * pscale_ref[:, c0:c0 + LANES]).astype(BF16)


def _branches(bc, halo, gvb, wsp, bsp_full, wpool, pscale, tm, csz, seq, pos0):
    T = bc.shape[0]
    row = lambda i: (i, 0)
    full2 = lambda i: (0, 0)
    full3 = lambda i: (0, 0, 0)
    return pl.pallas_call(
        functools.partial(_branches_kernel, tm=tm, csz=csz, seq=seq, pos0=pos0),
        grid=(T // tm,),
        in_specs=[pl.BlockSpec((tm, 2 * W_B + W_C), row),
                  pl.BlockSpec((1, HALO, W_C), lambda i: (i, 0, 0)),
                  pl.BlockSpec((1, W_B), full2),
                  pl.BlockSpec((N_GROUPS_B, csz, csz), full3),
                  pl.BlockSpec((csz, W_B), full2),
                  pl.BlockSpec((N_GROUPS_C, LANES, LANES), full3),
                  pl.BlockSpec((1, W_C), full2)],
        out_specs=(pl.BlockSpec((tm, W_B), row), pl.BlockSpec((tm, W_C), row),
                   pl.BlockSpec((tm, W_B), row)),
        out_shape=(jax.ShapeDtypeStruct((T, W_B), BF16), jax.ShapeDtypeStruct((T, W_C), BF16),
                   jax.ShapeDtypeStruct((T, W_B), F32)),
        compiler_params=_cparams(("arbitrary",)), name="branches",
    )(bc, halo, gvb, wsp, bsp_full, wpool, pscale)


def _pack_halves(x):
    c = x.shape[1] // 2
    lo = lax.bitcast_convert_type(x[:, :c].astype(BF16).astype(F32), jnp.uint32)
    hi = lax.bitcast_convert_type(x[:, c:].astype(BF16).astype(F32), jnp.uint32)
    return (lo >> 16) | (hi & jnp.uint32(0xFFFF0000))


def _unpack_halves(u):
    lo = lax.bitcast_convert_type(u << 16, F32)
    hi = lax.bitcast_convert_type(u & jnp.uint32(0xFFFF0000), F32)
    return lo, hi


def _merge_kernel(x_ref, oa_ref, ob_ref, oc_ref, gt_ref, wb_ref, wo_ref, g2_ref,
                  wrh_ref, wrl_ref, br_ref, tri_ref, h_ref, hp_ref, route_ref, route_t_ref, cnt_ref,
                  carry_ref):
    @pl.when(pl.program_id(0) == 0)
    def _():
        carry_ref[...] = jnp.zeros(carry_ref.shape, F32)

    ta = jnp.dot(oa_ref[...], wb_ref[0:W_A, :], preferred_element_type=F32)
    tb = jnp.dot(ob_ref[...], wb_ref[W_A:W_A + W_B, :], preferred_element_type=F32)
    tc = jnp.dot(oc_ref[...], wb_ref[W_A + W_B:W_A + W_B + W_C, :], preferred_element_type=F32)
    merged = (gt_ref[:, 0:D_MODEL].astype(F32) * ta
              + gt_ref[:, D_MODEL:2 * D_MODEL].astype(F32) * tb
              + gt_ref[:, 2 * D_MODEL:3 * D_MODEL].astype(F32) * tc)
    h = x_ref[...] + jnp.dot(merged.astype(BF16), wo_ref[...], preferred_element_type=F32)
    h_ref[...] = h
    ms = jnp.mean(h * h, axis=-1, keepdims=True)
    hn = h * lax.rsqrt(ms + EPS) * g2_ref[...]
    hp_ref[...] = _pack_halves(hn)

    hi = hn.astype(BF16)
    lo = (hn - hi.astype(F32)).astype(BF16)
    lg = (jnp.dot(hi, wrh_ref[...], preferred_element_type=F32)
          + jnp.dot(lo, wrh_ref[...], preferred_element_type=F32)
          + jnp.dot(hi, wrl_ref[...], preferred_element_type=F32)) + br_ref[...]
    lane = lax.broadcasted_iota(jnp.int32, lg.shape, 1).astype(F32)
    big = float(LANES)
    gmask = lane < N_EXPERT_GROUPS
    lgg = jnp.where(gmask, lg, NEG)
    mg = jnp.max(lgg, axis=-1, keepdims=True)
    sg = jnp.sum(jnp.where(gmask, jnp.exp(lgg - mg), 0.0), axis=-1, keepdims=True)
    g_p = 1.0 / sg
    g_sel = jnp.min(jnp.where(lgg == mg, lane, big), axis=-1, keepdims=True)
    e0 = ROUTER_OFF + g_sel * EXPERTS_PER_GROUP
    emask = (lane >= e0) & (lane < e0 + EXPERTS_PER_GROUP)
    le = jnp.where(emask, lg, NEG)
    me = jnp.max(le, axis=-1, keepdims=True)
    ee = jnp.where(emask, jnp.exp(le - me), 0.0)
    pe = ee / jnp.sum(ee, axis=-1, keepdims=True)
    pe = jnp.where(emask, pe, -1.0)
    top1 = jnp.max(pe, axis=-1, keepdims=True)
    i1 = jnp.min(jnp.where(pe == top1, lane, big), axis=-1, keepdims=True)
    pe2 = jnp.where(lane == i1, -1.0, pe)
    top2 = jnp.max(pe2, axis=-1, keepdims=True)
    i2 = jnp.min(jnp.where(pe2 == top2, lane, big), axis=-1, keepdims=True)
    den = top1 + top2
    w1 = g_p * (top1 / den)
    w2 = g_p * (top2 / den)

    oh1 = lane == i1
    oh2 = lane == i2
    oh = jnp.where(oh1 | oh2, 1.0, 0.0)
    prefix = carry_ref[...] + jnp.dot(tri_ref[...], oh.astype(BF16), preferred_element_type=F32)
    rank1 = jnp.sum(jnp.where(oh1, prefix, 0.0), axis=-1, keepdims=True)
    rank2 = jnp.sum(jnp.where(oh2, prefix, 0.0), axis=-1, keepdims=True)
    carry = carry_ref[...] + jnp.sum(oh, axis=0, keepdims=True)
    carry_ref[...] = carry
    cnt_ref[...] = carry
    li = lax.broadcasted_iota(jnp.int32, lg.shape, 1)
    fields = (i1 - ROUTER_OFF, i2 - ROUTER_OFF, rank1, rank2, w1, w2)
    route = jnp.zeros(lg.shape, F32)
    for k, f in enumerate(fields):
        route = jnp.where(li == k, f, route)
    route_ref[...] = route
    route_t_ref[...] = route.T[0:ROUTE_ROWS, :]


def _merge(x, oa, ob, oc, gt, wb, wo, g2, wrh, wrl, br, tri, tm):
    T = x.shape[0]
    row = lambda i: (i, 0)
    full = lambda i: (0, 0)
    const = lambda shape: pl.BlockSpec(shape, full, pipeline_mode=pl.Buffered(1))
    return pl.pallas_call(
        _merge_kernel, grid=(T // tm,),
        in_specs=[pl.BlockSpec((tm, D_MODEL), row), pl.BlockSpec((tm, W_A), row),
                  pl.BlockSpec((tm, W_B), row), pl.BlockSpec((tm, W_C), row),
                  pl.BlockSpec((tm, N_BRANCH * D_MODEL), row),
                  const((D_MODEL, D_MODEL)), const((D_MODEL, D_MODEL)),
                  pl.BlockSpec((1, D_MODEL), full),
                  const((D_MODEL, LANES)), const((D_MODEL, LANES)),
                  pl.BlockSpec((1, LANES), full), pl.BlockSpec((tm, tm), full)],
        out_specs=(pl.BlockSpec((tm, D_MODEL), row), pl.BlockSpec((tm, D_MODEL // 2), row),
                   pl.BlockSpec((tm, LANES), row), pl.BlockSpec((ROUTE_ROWS, tm), lambda i: (0, i)),
                   pl.BlockSpec((1, LANES), full)),
        out_shape=(jax.ShapeDtypeStruct((T, D_MODEL), F32),
                   jax.ShapeDtypeStruct((T, D_MODEL // 2), jnp.uint32),
                   jax.ShapeDtypeStruct((T, LANES), F32), jax.ShapeDtypeStruct((ROUTE_ROWS, T), F32),
                   jax.ShapeDtypeStruct((1, LANES), F32)),
        scratch_shapes=[pltpu.VMEM((1, LANES), F32)],
        compiler_params=_cparams(("arbitrary",)), name="merge",
    )(x, oa, ob, oc, gt, wb, wo, g2, wrh, wrl, br, tri)


def _row_copy(src_ref, src_row, dst_ref, dst_row, sem):
    return pltpu.make_async_copy(src_ref.at[pl.ds(src_row, 1)], dst_ref.at[pl.ds(dst_row, 1)], sem)


def _slot_row(offs_ref, e_ref, rk_ref, j):
    return offs_ref[e_ref[0, 0, j]] + rk_ref[0, 0, j]


def _wait_rows(src_ref, dst_ref, sem, n):
    def wait(r, c):
        _row_copy(src_ref, 0, dst_ref, 0, sem).wait()
        return c

    lax.fori_loop(0, n, wait, 0, unroll=DMA_UNROLL)


def _dispatch_kernel(offs_ref, e_ref, rk_ref, hp_ref, xs_in_ref, xs_ref, sem, *, tm):
    del xs_in_ref

    def start(r, c):
        _row_copy(hp_ref, r, xs_ref, _slot_row(offs_ref, e_ref, rk_ref, r), sem).start()
        _row_copy(hp_ref, r, xs_ref, _slot_row(offs_ref, e_ref, rk_ref, tm + r), sem).start()
        return c

    lax.fori_loop(0, tm, start, 0, unroll=DMA_UNROLL)
    _wait_rows(hp_ref, xs_ref, sem, 2 * tm)


def _slot_specs(tm):
    smem = lambda: pl.BlockSpec((1, 1, 2 * tm), lambda i, offs: (i, 0, 0), memory_space=pltpu.SMEM)
    return [smem(), smem()]


def _dispatch(offs, eidx, ridx, hp, xs0, tm):
    T, C = hp.shape
    grid_spec = pltpu.PrefetchScalarGridSpec(
        num_scalar_prefetch=1, grid=(T // tm,),
        in_specs=_slot_specs(tm) + [pl.BlockSpec((tm, C), lambda i, offs: (i, 0)),
                                    pl.BlockSpec(memory_space=pl.ANY)],
        out_specs=pl.BlockSpec(memory_space=pl.ANY),
        scratch_shapes=[pltpu.SemaphoreType.DMA(())],
    )
    return pl.pallas_call(
        functools.partial(_dispatch_kernel, tm=tm), grid_spec=grid_spec,
        out_shape=jax.ShapeDtypeStruct(xs0.shape, xs0.dtype),
        input_output_aliases={4: 0},
        compiler_params=_cparams(("arbitrary",)), name="moe_dispatch",
    )(offs, eidx, ridx, hp, xs0)


def _expert_kernel(te_ref, nv_ref, xs_ref, wg_ref, wu_ref, wd_ref, ys_ref):
    del te_ref
    t = pl.program_id(0)

    @pl.when(t < nv_ref[0])
    def _():
        lo, hi = _unpack_halves(xs_ref[...])
        lo = lo.astype(BF16)
        hi = hi.astype(BF16)
        half = D_MODEL // 2
        a = (jnp.dot(lo, wg_ref[0:half, :], preferred_element_type=F32)
             + jnp.dot(hi, wg_ref[half:D_MODEL, :], preferred_element_type=F32))
        b = (jnp.dot(lo, wu_ref[0:half, :], preferred_element_type=F32)
             + jnp.dot(hi, wu_ref[half:D_MODEL, :], preferred_element_type=F32))
        hh = ((a * jax.nn.sigmoid(a)) * b).astype(BF16)
        ys_ref[...] = _pack_halves(jnp.dot(hh, wd_ref[...], preferred_element_type=F32))

    @pl.when(t >= nv_ref[0])
    def _():
        ys_ref[...] = jnp.zeros(ys_ref.shape, ys_ref.dtype)


def _experts(tile_expert, n_valid, xs, wg, wu, wd, layer):
    NP, C = xs.shape
    wspec = lambda shape: pl.BlockSpec((None, None) + shape, lambda t, te, nv: (layer, te[t], 0, 0))
    grid_spec = pltpu.PrefetchScalarGridSpec(
        num_scalar_prefetch=2, grid=(NP // MOE_TM,),
        in_specs=[pl.BlockSpec((MOE_TM, C), lambda t, te, nv: (t, 0)),
                  wspec((D_MODEL, D_EXPERT)), wspec((D_MODEL, D_EXPERT)), wspec((D_EXPERT, D_MODEL))],
        out_specs=pl.BlockSpec((MOE_TM, C), lambda t, te, nv: (t, 0)),
    )
    return pl.pallas_call(
        _expert_kernel, grid_spec=grid_spec, out_shape=jax.ShapeDtypeStruct(xs.shape, xs.dtype),
        compiler_params=_cparams(("arbitrary",)), name="moe_experts",
    )(tile_expert, n_valid, xs, wg, wu, wd)


def _combine_kernel(offs_ref, e_ref, rk_ref, h_ref, route_ref, ys_ref, y_ref, r1_ref, r2_ref, sem,
                    *, tm):
    def start(r, c):
        _row_copy(ys_ref, _slot_row(offs_ref, e_ref, rk_ref, r), r1_ref, r, sem).start()
        _row_copy(ys_ref, _slot_row(offs_ref, e_ref, rk_ref, tm + r), r2_ref, r, sem).start()
        return c

    lax.fori_loop(0, tm, start, 0, unroll=DMA_UNROLL)
    _wait_rows(ys_ref, r1_ref, sem, 2 * tm)
    route = route_ref[...]
    w1 = route[:, 4:5]
    w2 = route[:, 5:6]
    a_lo, a_hi = _unpack_halves(r1_ref[...])
    b_lo, b_hi = _unpack_halves(r2_ref[...])
    half = D_MODEL // 2
    y_ref[:, 0:half] = h_ref[:, 0:half] + w1 * a_lo + w2 * b_lo
    y_ref[:, half:D_MODEL] = h_ref[:, half:D_MODEL] + w1 * a_hi + w2 * b_hi


def _combine(offs, eidx, ridx, h, route, ys, tm):
    T = h.shape[0]
    C = ys.shape[1]
    row = lambda i, offs: (i, 0)
    grid_spec = pltpu.PrefetchScalarGridSpec(
        num_scalar_prefetch=1, grid=(T // tm,),
        in_specs=_slot_specs(tm) + [pl.BlockSpec((tm, D_MODEL), row), pl.BlockSpec((tm, LANES), row),
                                    pl.BlockSpec(memory_space=pl.ANY)],
        out_specs=pl.BlockSpec((tm, D_MODEL), row),
        scratch_shapes=[pltpu.VMEM((tm, C), jnp.uint32), pltpu.VMEM((tm, C), jnp.uint32),
                        pltpu.SemaphoreType.DMA(())],
    )
    return pl.pallas_call(
        functools.partial(_combine_kernel, tm=tm), grid_spec=grid_spec,
        out_shape=jax.ShapeDtypeStruct((T, D_MODEL), F32),
        compiler_params=_cparams(("arbitrary",)), name="moe_combine",
    )(offs, eidx, ridx, h, route, ys)


def _moe_sparse(h, hp, route, route_t, counts, wg, wu, wd, layer, tm):
    T = h.shape[0]
    n_pad = 2 * T + N_EXPERTS * MOE_TM
    cnt = counts[0, ROUTER_OFF:ROUTER_OFF + N_EXPERTS].astype(jnp.int32)
    padded = ((cnt + MOE_TM - 1) // MOE_TM) * MOE_TM
    ends = jnp.cumsum(padded)
    offs = ends - padded
    tile_start = jnp.arange(n_pad // MOE_TM, dtype=jnp.int32) * MOE_TM
    tile_expert = jnp.minimum(jnp.sum(tile_start[:, None] >= ends[None, :], axis=1),
                              N_EXPERTS - 1).astype(jnp.int32)
    n_valid = (ends[-1] // MOE_TM).astype(jnp.int32).reshape(1)
    per_step = lambda rows: jnp.transpose(rows.astype(jnp.int32).reshape(2, T // tm, tm),
                                          (1, 0, 2)).reshape(T // tm, 1, 2 * tm)
    eidx = per_step(route_t[0:2])
    ridx = per_step(route_t[2:4])
    offs = offs.astype(jnp.int32)
    xs0 = jnp.zeros((n_pad, D_MODEL // 2), jnp.uint32)
    xs = _dispatch(offs, eidx, ridx, hp, xs0, tm)
    ys = _experts(tile_expert, n_valid, xs, wg, wu, wd, layer)
    return _combine(offs, eidx, ridx, h, route, ys, tm)


def _pick(n, cands):
    for c in cands:
        if n % c == 0:
            return c
    raise ValueError(f"no tile for {n}")


def _layer(x, l, P, w, cache_k, cache_v, pool_prev, B, L, is_prompt):
    T = B * L
    lam_init = 0.8 - 0.6 * math.exp(-0.3 * l)
    tm = _pick(T, (1024, 512, 256))
    q, kf, kb, vf, vb, bc, xn = _inproj(x, P["norm1"], w["w_in"], w["e_mat"], P["gq"], P["gk"], tm)
    gt = _gates(xn, w["w_gates"], tm)

    if is_prompt:
        tq = _pick(L, (2048, 1024, 512, 256, 128, 64))
        oa = _attn_prompt(q, kb, vb, P["lam"], P["g_sub"], P["attn_fast"], B, L, tq, lam_init)
    else:
        oa = _attn_sample(q, kb, vb, cache_k, cache_v, l, P["lam"], P["g_sub"], B, L, lam_init)

    xc = bc[:, 2 * W_B:].reshape(B, L, W_C)
    if is_prompt:
        tmb = _pick(L, (512, 256, 128))
        csz = SG_CHUNK
        nt = L // tmb
        tails = xc.reshape(B, nt, tmb, W_C)[:, :, tmb - HALO:, :]
        halo = jnp.concatenate([jnp.zeros((B, 1, HALO, W_C), F32), tails[:, :-1]], axis=1)
        halo = halo.reshape(B * nt, HALO, W_C)
        pos0 = 0
        pool_new = xc[:, L - POOL_STATE:, :]
    else:
        tmb = L
        csz = L
        halo = jnp.concatenate([jnp.zeros((B, HALO - POOL_STATE, W_C), F32), pool_prev], axis=1)
        pos0 = cache_k.shape[2]
        pool_new = jnp.concatenate([pool_prev, xc], axis=1)[:, -POOL_STATE:, :]
    ob, oc, vbn = _branches(bc, halo, P["g_vb"], w["w_sp"][:, :csz, :csz], w["b_sp_full"][:csz],
                            w["w_pool"], P["pool_scale"], tmb, csz, L, pos0)

    tm2 = _pick(T, (MERGE_TM,))
    h, hp, route, route_t, counts = _merge(x, oa, ob, oc, gt, w["w_branch"], w["w_out"], P["norm2"],
                                           w["wr_hi"], w["wr_lo"], P["b_r"], w["tri_rank"], tm2)
    y = _moe_sparse(h, hp, route, route_t, counts, w["w_e_gate"], w["w_e_up"], w["w_e_down"], l, tm2)
    return y, kf, vf, pool_new, vbn


def kernel(x_prompt, x_sample, cache_k, cache_v, state_pool, norm1, w_in, g_q, g_k, lam, g_sub,
           g_vb, w_sp, b_sp, w_pool, pool_scale, w_branch, w_out, norm2, w_rg, b_rg, w_re, b_re,
           w_e_gate, w_e_up, w_e_down):
    depth = w_in.shape[0]
    B, S, D = x_prompt.shape
    Bs, Ls, _ = x_sample.shape
    PL = cache_k.shape[2]
    ck = cache_k.reshape(depth, Bs, PL, W_A)
    cv = cache_v.reshape(depth, Bs, PL, W_A)

    rep = IN_TN // HD_A
    gidx = jnp.arange(IN_TN) // HD_A
    e_mat = (gidx[:, None] == gidx[None, :]).astype(BF16)
    tri = jnp.tril(jnp.ones((SG_CHUNK, SG_CHUNK), bool))
    weg, weu, wed = w_e_gate.astype(BF16), w_e_up.astype(BF16), w_e_down.astype(BF16)
    ridx = jnp.arange(MERGE_TM)
    tri_rank = (ridx[None, :] < ridx[:, None]).astype(BF16)

    xp = x_prompt.reshape(B * S, D)
    xs = x_sample.reshape(Bs * Ls, D)
    outs = {k: [] for k in ("kp", "vp", "pp", "ks", "vs", "ps", "gs")}
    for l in range(depth):
        w_r = jnp.zeros((D, LANES), F32)
        w_r = w_r.at[:, :N_EXPERT_GROUPS].set(w_rg[l]).at[:, ROUTER_OFF:ROUTER_OFF + N_EXPERTS].set(w_re[l])
        wr_hi = w_r.astype(BF16)
        wr_lo = (w_r - wr_hi.astype(F32)).astype(BF16)
        b_r = jnp.zeros((1, LANES), F32)
        b_r = b_r.at[0, :N_EXPERT_GROUPS].set(b_rg[l]).at[0, ROUTER_OFF:ROUTER_OFF + N_EXPERTS].set(b_re[l])
        score_bound = HD_A * QK_SCALE * jnp.max(jnp.abs(g_q[l])) * jnp.max(jnp.abs(g_k[l]))
        attn_fast = (score_bound <= MAX_UNSHIFTED_SCORE).astype(jnp.int32).reshape(1)
        P = dict(norm1=norm1[l][None], norm2=norm2[l][None], attn_fast=attn_fast,
                 gq=(jnp.tile(g_q[l], rep) * QK_SCALE)[None], gk=jnp.tile(g_k[l], rep)[None],
                 lam=lam[l], g_sub=g_sub[l][None], g_vb=g_vb[l][None],
                 pool_scale=pool_scale[l][None], b_r=b_r)
        w = dict(w_in=w_in[l, :, :GATE_COL0].astype(BF16), w_gates=w_in[l, :, GATE_COL0:].astype(BF16),
                 e_mat=e_mat,
                 w_sp=jnp.where(tri, w_sp[l], 0.0).astype(BF16),
                 b_sp_full=jnp.repeat(jnp.transpose(b_sp[l]), LANES, axis=1),
                 w_pool=w_pool[l].astype(BF16), w_branch=w_branch[l].astype(BF16),
                 w_out=w_out[l].astype(BF16), wr_hi=wr_hi, wr_lo=wr_lo,
                 w_e_gate=weg, w_e_up=weu, w_e_down=wed, tri_rank=tri_rank)
        xp, k1, v1, pool1, _ = _layer(xp, l, P, w, None, None, None, B, S, True)
        xs, k2, v2, pool2, vb2 = _layer(xs, l, P, w, ck, cv, state_pool[l], Bs, Ls, False)
        outs["kp"].append(k1.reshape(B, S, N_HEADS_A, 2, HD_A))
        outs["vp"].append(v1.reshape(B, S, N_HEADS_A, 2 * HD_A))
        outs["pp"].append(pool1)
        outs["ks"].append(k2.reshape(Bs, Ls, N_HEADS_A, 2, HD_A))
        outs["vs"].append(v2.reshape(Bs, Ls, N_HEADS_A, 2 * HD_A))
        outs["ps"].append(pool2)
        outs["gs"].append(vb2.reshape(Bs, Ls, W_B))
    st = lambda k: jnp.stack(outs[k])
    return (xp.reshape(B, S, D), xs.reshape(Bs, Ls, D), st("kp"), st("vp"), st("pp"),
            st("ks"), st("vs"), st("ps"), st("gs"))
```

```python
import functools
import math

import jax
import jax.numpy as jnp
from jax import lax
from jax.experimental import pallas as pl
from jax.experimental.pallas import tpu as pltpu

F32 = jnp.float32
BF16 = jnp.bfloat16

D_MODEL = 2048
CHUNK = 64
N_HEADS_A = 8
HD_A = 64
W_A = N_HEADS_A * 2 * HD_A
SG_CHUNK = 128
N_GROUPS_B = 4
W_B = 512
POOL_WINDOWS = (2, 4, 8, 16)
N_GROUPS_C = 4
W_C = 512
POOL_STATE = 15
N_BRANCH = 3
IN_COLS = W_A * 3 + W_B * 2 + W_C + N_BRANCH * D_MODEL
N_EXPERT_GROUPS = 4
EXPERTS_PER_GROUP = 8
N_EXPERTS = N_EXPERT_GROUPS * EXPERTS_PER_GROUP
D_EXPERT = 256
EPS = 1e-6

LANES = 128
HALO = 16
NEG = -1e30
ATTN_RBLK = 512
QK_SCALE = (HD_A ** -0.5) * math.log2(math.e)
MAX_UNSHIFTED_SCORE = 48.0
VMEM_LIMIT = 58 * 1024 * 1024
ROUTER_OFF = N_EXPERT_GROUPS

IN_TN = 512
_JQ, _JK, _JV, _JBC, _JG = 0, 2, 4, 6, 9
GATE_COL0 = _JG * IN_TN
GATES_TN = 1024
GATES_RBLK = 256
INPROJ_RBLK = 512
MERGE_TM = 256
MERGE_RBLK = 256
MOE_TM = 256
ROUTE_ROWS = 8
DMA_UNROLL = 8


def _cparams(sem):
    return pltpu.CompilerParams(dimension_semantics=sem, vmem_limit_bytes=VMEM_LIMIT)


def _inproj_kernel(x_ref, g1_ref, w_ref, e_ref, gq_ref, gk_ref, k_all_in, v_all_in,
                   q_ref, kf_ref, kb_ref, vf_ref, vb_ref, bc_ref, xn_ref, *, tm, rblk):
    del k_all_in, v_all_in
    j = pl.program_id(1)

    @pl.when(j == 0)
    def _():
        x = x_ref[...]
        ms = jnp.mean(x * x, axis=-1, keepdims=True)
        xn_ref[...] = (x * lax.rsqrt(ms + EPS) * g1_ref[...]).astype(BF16)

    def headnorm(p, g):
        ss = jnp.dot((p * p).astype(BF16), e_ref[...], preferred_element_type=F32)
        return p * lax.rsqrt(ss * (1.0 / HD_A) + EPS) * g

    def row_blocks(epilogue):
        def run():
            for r0 in range(0, tm, rblk):
                rows = slice(r0, r0 + rblk)
                epilogue(rows, jnp.dot(xn_ref[rows, :], w_ref[...], preferred_element_type=F32))
        return run

    def q_out(rows, p):
        q_ref[rows, :] = headnorm(p, gq_ref[...]).astype(BF16)

    def k_out(rows, p):
        kn = headnorm(p, gk_ref[...])
        kf_ref[rows, :] = kn
        kb_ref[rows, :] = kn.astype(BF16)

    def v_out(rows, p):
        vf_ref[rows, :] = p
        vb_ref[rows, :] = p.astype(BF16)

    def bc_out(rows, p):
        bc_ref[rows, :] = p

    pl.when(j < _JK)(row_blocks(q_out))
    pl.when((j >= _JK) & (j < _JV))(row_blocks(k_out))
    pl.when((j >= _JV) & (j < _JBC))(row_blocks(v_out))
    pl.when(j >= _JBC)(row_blocks(bc_out))


def _gates_kernel(xn_ref, w_ref, gt_ref, *, tm, rblk):
    for r0 in range(0, tm, rblk):
        p = jnp.dot(xn_ref[r0:r0 + rblk, :], w_ref[...], preferred_element_type=F32)
        gt_ref[r0:r0 + rblk, :] = (0.5 * jnp.tanh(0.5 * p) + 0.5).astype(BF16)


def _gates(xn, w_g, tm):
    T = xn.shape[0]
    ncol = w_g.shape[1]
    return pl.pallas_call(
        functools.partial(_gates_kernel, tm=tm, rblk=min(GATES_RBLK, tm)),
        grid=(T // tm, ncol // GATES_TN),
        in_specs=[pl.BlockSpec((tm, D_MODEL), lambda i, j: (i, 0)),
                  pl.BlockSpec((D_MODEL, GATES_TN), lambda i, j: (0, j))],
        out_specs=pl.BlockSpec((tm, GATES_TN), lambda i, j: (i, j)),
        out_shape=jax.ShapeDtypeStruct((T, ncol), BF16),
        compiler_params=_cparams(("arbitrary", "arbitrary")), name="gates",
    )(xn, w_g)


def _inproj(x, g1, w_bf, e_mat, gq, gk, kv_all, layer, depth, tm):
    T = x.shape[0]
    grid = (T // tm, _JG)

    def seg(j0, n):
        return lambda i, j: (i, jnp.clip(j - j0, 0, n - 1))

    def seg_l(j0, n):
        return lambda i, j: (layer, i, jnp.clip(j - j0, 0, n - 1))

    out_shape = (
        jax.ShapeDtypeStruct((T, W_A), BF16),
        jax.ShapeDtypeStruct((depth, T, W_A), F32),
        jax.ShapeDtypeStruct((T, W_A), BF16),
        jax.ShapeDtypeStruct((depth, T, W_A), F32),
        jax.ShapeDtypeStruct((T, W_A), BF16),
        jax.ShapeDtypeStruct((T, 2 * W_B + W_C), F32),
        jax.ShapeDtypeStruct((T, D_MODEL), BF16),
    )
    blk = lambda f: pl.BlockSpec((tm, IN_TN), f)
    blk_l = lambda f: pl.BlockSpec((None, tm, IN_TN), f)
    out_specs = (blk(seg(_JQ, 2)), blk_l(seg_l(_JK, 2)), blk(seg(_JK, 2)), blk_l(seg_l(_JV, 2)),
                 blk(seg(_JV, 2)), blk(seg(_JBC, 3)),
                 pl.BlockSpec((tm, D_MODEL), lambda i, j: (i, 0)))
    in_specs = [
        pl.BlockSpec((tm, D_MODEL), lambda i, j: (i, 0)),
        pl.BlockSpec((1, D_MODEL), lambda i, j: (0, 0)),
        pl.BlockSpec((D_MODEL, IN_TN), lambda i, j: (0, j)),
        pl.BlockSpec((IN_TN, IN_TN), lambda i, j: (0, 0)),
        pl.BlockSpec((1, IN_TN), lambda i, j: (0, 0)),
        pl.BlockSpec((1, IN_TN), lambda i, j: (0, 0)),
    ]
    in_specs += [pl.BlockSpec(memory_space=pl.ANY)] * 2
    return pl.pallas_call(
        functools.partial(_inproj_kernel, tm=tm, rblk=min(INPROJ_RBLK, tm)),
        grid=grid, in_specs=in_specs, out_specs=out_specs, out_shape=out_shape,
        input_output_aliases={6: 1, 7: 3},
        compiler_params=_cparams(("arbitrary", "arbitrary")), name="inproj",
    )(x, g1, w_bf, e_mat, gq, gk, *kv_all)


def _lambda_value(lam_ref, lam_init):
    lp = lam_ref[...]
    a = jnp.sum(lp[0:1] * lp[1:2], axis=-1, keepdims=True)
    b = jnp.sum(lp[2:3] * lp[3:4], axis=-1, keepdims=True)
    return jnp.exp(a) - jnp.exp(b) + lam_init


def _stack_q(q, qs_ref, tq):
    lane = lax.broadcasted_iota(jnp.int32, q.shape, 1)
    zero = jnp.zeros_like(q)
    qs_ref[0:tq, :] = jnp.where(lane < HD_A, q, zero)
    qs_ref[tq:2 * tq, :] = jnp.where(lane >= HD_A, q, zero)


def _attn_finish(acc, l, lam, gsub, tq, lam_init):
    o = acc / l
    o = o[0:tq] - lam * o[tq:2 * tq]
    ms = jnp.mean(o * o, axis=-1, keepdims=True)
    return o * lax.rsqrt(ms + EPS) * gsub * (1.0 - lam_init)


def _attn_prompt_kernel(qi_ref, kj_ref, fast_ref, q_ref, k_ref, v_ref, lam_ref, gsub_ref, o_ref,
                        qs_ref, m_ref, l_ref, acc_ref, *, tq, rblk, lam_init):
    t = pl.program_id(2)
    qi = qi_ref[t]
    kj = kj_ref[t]
    fast = fast_ref[0] != 0

    @pl.when(kj == 0)
    def _():
        _stack_q(q_ref[...], qs_ref, tq)
        m_ref[...] = jnp.full(m_ref.shape, NEG, F32)
        l_ref[...] = jnp.zeros(l_ref.shape, F32)
        acc_ref[...] = jnp.zeros(acc_ref.shape, F32)

    def step(masked, bounded):
        for rb in range(2 * tq // rblk):
            r0 = rb * rblk
            q0 = r0 % tq
            nk = min(tq, q0 + rblk) if masked else tq
            s = lax.dot_general(qs_ref[r0:r0 + rblk, :], k_ref[0:nk, :], (((1,), (1,)), ((), ())),
                                preferred_element_type=F32)
            if masked:
                r = lax.broadcasted_iota(jnp.int32, s.shape, 0) + q0
                c = lax.broadcasted_iota(jnp.int32, s.shape, 1)
                s = jnp.where((c // CHUNK) <= (r // CHUNK), s, NEG)
            if bounded:
                p = jnp.exp2(s)
                l_ref[r0:r0 + rblk, :] += jnp.sum(p, axis=-1, keepdims=True)
                acc_ref[r0:r0 + rblk, :] += jnp.dot(p.astype(BF16), v_ref[0:nk, :],
                                                    preferred_element_type=F32)
                continue
            m_prev = m_ref[r0:r0 + rblk, :]
            m_new = jnp.maximum(m_prev, jnp.max(s, axis=-1, keepdims=True))
            alpha = jnp.exp2(m_prev - m_new)
            p = jnp.exp2(s - jnp.tile(m_new, (1, nk // LANES)))
            l_ref[r0:r0 + rblk, :] = alpha * l_ref[r0:r0 + rblk, :] + jnp.sum(p, axis=-1, keepdims=True)
            acc_ref[r0:r0 + rblk, :] = alpha * acc_ref[r0:r0 + rblk, :] + jnp.dot(
                p.astype(BF16), v_ref[0:nk, :], preferred_element_type=F32)
            m_ref[r0:r0 + rblk, :] = m_new

    for bounded in (True, False):
        use = fast if bounded else jnp.logical_not(fast)
        pl.when((kj < qi) & use)(functools.partial(step, False, bounded))
        pl.when((kj == qi) & use)(functools.partial(step, True, bounded))

    @pl.when(kj == qi)
    def _():
        lam = _lambda_value(lam_ref, lam_init)
        o_ref[...] = _attn_finish(acc_ref[...], l_ref[...], lam, gsub_ref[...], tq,
                                  lam_init).astype(o_ref.dtype)


def _attn_prompt(q, k, v, lam_p, gsub, fast, B, S, tq, lam_init):
    nq = S // tq
    pairs = [(i, j) for i in range(nq) for j in range(i + 1)]
    qi = jnp.asarray([p[0] for p in pairs], jnp.int32)
    kj = jnp.asarray([p[1] for p in pairs], jnp.int32)
    grid_spec = pltpu.PrefetchScalarGridSpec(
        num_scalar_prefetch=3,
        grid=(B, N_HEADS_A, len(pairs)),
        in_specs=[
            pl.BlockSpec((tq, LANES), lambda b, h, t, qi, kj, f: (b * nq + qi[t], h)),
            pl.BlockSpec((tq, LANES), lambda b, h, t, qi, kj, f: (b * nq + kj[t], h)),
            pl.BlockSpec((tq, LANES), lambda b, h, t, qi, kj, f: (b * nq + kj[t], h)),
            pl.BlockSpec((4, HD_A), lambda b, h, t, qi, kj, f: (0, 0)),
            pl.BlockSpec((1, LANES), lambda b, h, t, qi, kj, f: (0, 0)),
        ],
        out_specs=pl.BlockSpec((tq, LANES), lambda b, h, t, qi, kj, f: (b * nq + qi[t], h)),
        scratch_shapes=[pltpu.VMEM((2 * tq, LANES), BF16), pltpu.VMEM((2 * tq, LANES), F32),
                        pltpu.VMEM((2 * tq, LANES), F32), pltpu.VMEM((2 * tq, LANES), F32)],
    )
    return pl.pallas_call(
        functools.partial(_attn_prompt_kernel, tq=tq, rblk=min(ATTN_RBLK, tq), lam_init=lam_init),
        grid_spec=grid_spec, out_shape=jax.ShapeDtypeStruct((B * S, W_A), BF16),
        compiler_params=_cparams(("arbitrary", "arbitrary", "arbitrary")), name="attn_prompt",
    )(qi, kj, fast, q, k, v, lam_p, gsub)


def _attn_sample_kernel(q_ref, kc_ref, vc_ref, kn_ref, vn_ref, lam_ref, gsub_ref, o_ref,
                        qs_ref, *, tq, lam_init):
    _stack_q(q_ref[...], qs_ref, tq)
    qs = qs_ref[...]
    dn = (((1,), (1,)), ((), ()))
    s_c = lax.dot_general(qs, kc_ref[...].astype(BF16), dn, preferred_element_type=F32)
    s_n = lax.dot_general(qs, kn_ref[...], dn, preferred_element_type=F32)
    m = jnp.maximum(jnp.max(s_c, axis=-1, keepdims=True), jnp.max(s_n, axis=-1, keepdims=True))
    p_c = jnp.exp2(s_c - m)
    p_n = jnp.exp2(s_n - m)
    l = jnp.sum(p_c, axis=-1, keepdims=True) + jnp.sum(p_n, axis=-1, keepdims=True)
    acc = (jnp.dot(p_c.astype(BF16), vc_ref[...].astype(BF16), preferred_element_type=F32)
           + jnp.dot(p_n.astype(BF16), vn_ref[...], preferred_element_type=F32))
    lam = _lambda_value(lam_ref, lam_init)
    o_ref[...] = _attn_finish(acc, l, lam, gsub_ref[...], tq, lam_init).astype(o_ref.dtype)


def _attn_sample(q, kn, vn, cache_k, cache_v, layer, lam_p, gsub, B, L, lam_init):
    P = cache_k.shape[2]
    cspec = pl.BlockSpec((None, None, P, LANES), lambda b, h: (layer, b, 0, h))
    nspec = pl.BlockSpec((L, LANES), lambda b, h: (b, h))
    return pl.pallas_call(
        functools.partial(_attn_sample_kernel, tq=L, lam_init=lam_init),
        grid=(B, N_HEADS_A),
        in_specs=[nspec, cspec, cspec, nspec, nspec,
                  pl.BlockSpec((4, HD_A), lambda b, h: (0, 0)),
                  pl.BlockSpec((1, LANES), lambda b, h: (0, 0))],
        out_specs=nspec,
        out_shape=jax.ShapeDtypeStruct((B * L, W_A), BF16),
        scratch_shapes=[pltpu.VMEM((2 * L, LANES), BF16)],
        compiler_params=_cparams(("arbitrary", "arbitrary")), name="attn_sample",
    )(q, cache_k, cache_v, kn, vn, lam_p, gsub)


def _branches_kernel(bc_ref, halo_ref, gvb_ref, wsp_ref, bsp_ref, wpool_ref, pscale_ref,
                     ob_ref, oc_ref, vbn_ref, *, tm, csz, seq, pos0):
    i = pl.program_id(0)
    u = bc_ref[:, 0:W_B]
    vb = bc_ref[:, W_B:2 * W_B]
    xc = bc_ref[:, 2 * W_B:2 * W_B + W_C]

    ms = jnp.mean(vb * vb, axis=-1, keepdims=True)
    vbn = vb * lax.rsqrt(ms + EPS) * gvb_ref[...]
    vbn_ref[...] = vbn
    vbn_bf = vbn.astype(BF16)
    for c in range(tm // csz):
        r0 = c * csz
        for g in range(N_GROUPS_B):
            c0 = g * LANES
            z = jnp.dot(wsp_ref[g], vbn_bf[r0:r0 + csz, c0:c0 + LANES],
                        preferred_element_type=F32) + bsp_ref[:, c0:c0 + LANES]
            ob_ref[r0:r0 + csz, c0:c0 + LANES] = (u[r0:r0 + csz, c0:c0 + LANES] * z).astype(BF16)

    ext = jnp.concatenate([halo_ref[0], xc], axis=0)
    row = lax.broadcasted_iota(jnp.int32, (tm, 1), 0)
    pos = (pos0 + (i * tm) % seq + row).astype(F32)
    acc = ext
    shift = 1
    for g, w in enumerate(POOL_WINDOWS):
        while shift < w:
            acc = acc + pltpu.roll(acc, shift, axis=0)
            shift *= 2
        c0 = g * LANES
        cnt = jnp.minimum(float(w), pos + 1.0)
        pooled = acc[HALO:HALO + tm, c0:c0 + LANES] / cnt
        diff = (pooled - xc[:, c0:c0 + LANES]).astype(BF16)
        h = jnp.dot(diff, wpool_ref[g], preferred_element_type=F32)
        oc_ref[:, c0:c0 + LANES] = (h * pscale_ref[:, c0:c0 + LANES]).astype(BF16)


def _branches(bc, halo, gvb, wsp, bsp_full, wpool, pscale, tm, csz, seq, pos0):
    T = bc.shape[0]
    row = lambda i: (i, 0)
    full2 = lambda i: (0, 0)
    full3 = lambda i: (0, 0, 0)
    return pl.pallas_call(
        functools.partial(_branches_kernel, tm=tm, csz=csz, seq=seq, pos0=pos0),
        grid=(T // tm,),
        in_specs=[pl.BlockSpec((tm, 2 * W_B + W_C), row),
                  pl.BlockSpec((1, HALO, W_C), lambda i: (i, 0, 0)),
                  pl.BlockSpec((1, W_B), full2),
                  pl.BlockSpec((N_GROUPS_B, csz, csz), full3),
                  pl.BlockSpec((csz, W_B), full2),
                  pl.BlockSpec((N_GROUPS_C, LANES, LANES), full3),
                  pl.BlockSpec((1, W_C), full2)],
        out_specs=(pl.BlockSpec((tm, W_B), row), pl.BlockSpec((tm, W_C), row),
                   pl.BlockSpec((tm, W_B), row)),
        out_shape=(jax.ShapeDtypeStruct((T, W_B), BF16), jax.ShapeDtypeStruct((T, W_C), BF16),
                   jax.ShapeDtypeStruct((T, W_B), F32)),
        compiler_params=_cparams(("arbitrary",)), name="branches",
    )(bc, halo, gvb, wsp, bsp_full, wpool, pscale)


def _pack_halves(x):
    c = x.shape[1] // 2
    lo = lax.bitcast_convert_type(x[:, :c].astype(BF16).astype(F32), jnp.uint32)
    hi = lax.bitcast_convert_type(x[:, c:].astype(BF16).astype(F32), jnp.uint32)
    return (lo >> 16) | (hi & jnp.uint32(0xFFFF0000))


def _unpack_halves(u):
    lo = lax.bitcast_convert_type(u << 16, F32)
    hi = lax.bitcast_convert_type(u & jnp.uint32(0xFFFF0000), F32)
    return lo, hi


def _merge_kernel(x_ref, oa_ref, ob_ref, oc_ref, gt_ref, wb_ref, wo_ref, g2_ref,
                  wrh_ref, wrl_ref, br_ref, tri_ref, h_ref, hp_ref, route_ref, route_t_ref, cnt_ref,
                  carry_ref, *, tm, rblk):
    @pl.when(pl.program_id(0) == 0)
    def _():
        carry_ref[...] = jnp.zeros(carry_ref.shape, F32)

    for r0 in range(0, tm, rblk):
        _merge_rows(slice(r0, r0 + rblk), rblk, x_ref, oa_ref, ob_ref, oc_ref, gt_ref, wb_ref, wo_ref,
                    g2_ref, wrh_ref, wrl_ref, br_ref, tri_ref, h_ref, hp_ref, route_ref, route_t_ref,
                    cnt_ref, carry_ref)


def _merge_rows(rows, rblk, x_ref, oa_ref, ob_ref, oc_ref, gt_ref, wb_ref, wo_ref, g2_ref,
                wrh_ref, wrl_ref, br_ref, tri_ref, h_ref, hp_ref, route_ref, route_t_ref, cnt_ref,
                carry_ref):
    ta = jnp.dot(oa_ref[rows, :], wb_ref[0:W_A, :], preferred_element_type=F32)
    tb = jnp.dot(ob_ref[rows, :], wb_ref[W_A:W_A + W_B, :], preferred_element_type=F32)
    tc = jnp.dot(oc_ref[rows, :], wb_ref[W_A + W_B:W_A + W_B + W_C, :], preferred_element_type=F32)
    merged = (gt_ref[rows, 0:D_MODEL].astype(F32) * ta
              + gt_ref[rows, D_MODEL:2 * D_MODEL].astype(F32) * tb
              + gt_ref[rows, 2 * D_MODEL:3 * D_MODEL].astype(F32) * tc)
    h = x_ref[rows, :] + jnp.dot(merged.astype(BF16), wo_ref[...], preferred_element_type=F32)
    h_ref[rows, :] = h
    ms = jnp.mean(h * h, axis=-1, keepdims=True)
    hn = h * lax.rsqrt(ms + EPS) * g2_ref[...]
    hp_ref[rows, :] = _pack_halves(hn)

    hi = hn.astype(BF16)
    lo = (hn - hi.astype(F32)).astype(BF16)
    lg = (jnp.dot(hi, wrh_ref[...], preferred_element_type=F32)
          + jnp.dot(lo, wrh_ref[...], preferred_element_type=F32)
          + jnp.dot(hi, wrl_ref[...], preferred_element_type=F32)) + br_ref[...]
    lane = lax.broadcasted_iota(jnp.int32, lg.shape, 1).astype(F32)
    big = float(LANES)
    gmask = lane < N_EXPERT_GROUPS
    lgg = jnp.where(gmask, lg, NEG)
    mg = jnp.max(lgg, axis=-1, keepdims=True)
    sg = jnp.sum(jnp.where(gmask, jnp.exp(lgg - mg), 0.0), axis=-1, keepdims=True)
    g_p = 1.0 / sg
    g_sel = jnp.min(jnp.where(lgg == mg, lane, big), axis=-1, keepdims=True)
    e0 = ROUTER_OFF + g_sel * EXPERTS_PER_GROUP
    emask = (lane >= e0) & (lane < e0 + EXPERTS_PER_GROUP)
    le = jnp.where(emask, lg, NEG)
    me = jnp.max(le, axis=-1, keepdims=True)
    ee = jnp.where(emask, jnp.exp(le - me), 0.0)
    pe = ee / jnp.sum(ee, axis=-1, keepdims=True)
    pe = jnp.where(emask, pe, -1.0)
    top1 = jnp.max(pe, axis=-1, keepdims=True)
    i1 = jnp.min(jnp.where(pe == top1, lane, big), axis=-1, keepdims=True)
    pe2 = jnp.where(lane == i1, -1.0, pe)
    top2 = jnp.max(pe2, axis=-1, keepdims=True)
    i2 = jnp.min(jnp.where(pe2 == top2, lane, big), axis=-1, keepdims=True)
    den = top1 + top2
    w1 = g_p * (top1 / den)
    w2 = g_p * (top2 / den)

    oh1 = lane == i1
    oh2 = lane == i2
    oh = jnp.where(oh1 | oh2, 1.0, 0.0)
    prefix = carry_ref[...] + jnp.dot(tri_ref[0:rblk, 0:rblk], oh.astype(BF16),
                                      preferred_element_type=F32)
    rank1 = jnp.sum(jnp.where(oh1, prefix, 0.0), axis=-1, keepdims=True)
    rank2 = jnp.sum(jnp.where(oh2, prefix, 0.0), axis=-1, keepdims=True)
    carry = carry_ref[...] + jnp.sum(oh, axis=0, keepdims=True)
    carry_ref[...] = carry
    cnt_ref[...] = carry
    li = lax.broadcasted_iota(jnp.int32, lg.shape, 1)
    fields = (i1 - ROUTER_OFF, i2 - ROUTER_OFF, rank1, rank2, w1, w2)
    route = jnp.zeros(lg.shape, F32)
    for k, f in enumerate(fields):
        route = jnp.where(li == k, f, route)
    route_ref[rows, :] = route
    route_t_ref[:, rows] = route.T[0:ROUTE_ROWS, :]


def _merge(x, oa, ob, oc, gt, wb, wo, g2, wrh, wrl, br, tri, tm):
    T = x.shape[0]
    row = lambda i: (i, 0)
    full = lambda i: (0, 0)
    const = lambda shape: pl.BlockSpec(shape, full, pipeline_mode=pl.Buffered(1))
    return pl.pallas_call(
        functools.partial(_merge_kernel, tm=tm, rblk=min(MERGE_RBLK, tm)), grid=(T // tm,),
        in_specs=[pl.BlockSpec((tm, D_MODEL), row), pl.BlockSpec((tm, W_A), row),
                  pl.BlockSpec((tm, W_B), row), pl.BlockSpec((tm, W_C), row),
                  pl.BlockSpec((tm, N_BRANCH * D_MODEL), row),
                  const((D_MODEL, D_MODEL)), const((D_MODEL, D_MODEL)),
                  pl.BlockSpec((1, D_MODEL), full),
                  const((D_MODEL, LANES)), const((D_MODEL, LANES)),
                  pl.BlockSpec((1, LANES), full), pl.BlockSpec((tm, tm), full)],
        out_specs=(pl.BlockSpec((tm, D_MODEL), row), pl.BlockSpec((tm, D_MODEL // 2), row),
                   pl.BlockSpec((tm, LANES), row), pl.BlockSpec((ROUTE_ROWS, tm), lambda i: (0, i)),
                   pl.BlockSpec((1, LANES), full)),
        out_shape=(jax.ShapeDtypeStruct((T, D_MODEL), F32),
                   jax.ShapeDtypeStruct((T, D_MODEL // 2), jnp.uint32),
                   jax.ShapeDtypeStruct((T, LANES), F32), jax.ShapeDtypeStruct((ROUTE_ROWS, T), F32),
                   jax.ShapeDtypeStruct((1, LANES), F32)),
        scratch_shapes=[pltpu.VMEM((1, LANES), F32)],
        compiler_params=_cparams(("arbitrary",)), name="merge",
    )(x, oa, ob, oc, gt, wb, wo, g2, wrh, wrl, br, tri)


def _row_copy(src_ref, src_row, dst_ref, dst_row, sem):
    return pltpu.make_async_copy(src_ref.at[pl.ds(src_row, 1)], dst_ref.at[pl.ds(dst_row, 1)], sem)


def _slot_row(offs_ref, e_ref, rk_ref, j):
    return offs_ref[e_ref[0, 0, j]] + rk_ref[0, 0, j]


def _wait_rows(src_ref, dst_ref, sem, n):
    def wait(r, c):
        _row_copy(src_ref, 0, dst_ref, 0, sem).wait()
        return c

    lax.fori_loop(0, n, wait, 0, unroll=DMA_UNROLL)


def _dispatch_kernel(offs_ref, e_ref, rk_ref, hp_ref, xs_in_ref, xs_ref, sem, *, tm):
    del xs_in_ref

    def start(r, c):
        _row_copy(hp_ref, r, xs_ref, _slot_row(offs_ref, e_ref, rk_ref, r), sem).start()
        _row_copy(hp_ref, r, xs_ref, _slot_row(offs_ref, e_ref, rk_ref, tm + r), sem).start()
        return c

    lax.fori_loop(0, tm, start, 0, unroll=DMA_UNROLL)
    _wait_rows(hp_ref, xs_ref, sem, 2 * tm)


def _slot_specs(tm):
    smem = lambda: pl.BlockSpec((1, 1, 2 * tm), lambda i, offs: (i, 0, 0), memory_space=pltpu.SMEM)
    return [smem(), smem()]


def _dispatch(offs, eidx, ridx, hp, xs0, tm):
    T, C = hp.shape
    grid_spec = pltpu.PrefetchScalarGridSpec(
        num_scalar_prefetch=1, grid=(T // tm,),
        in_specs=_slot_specs(tm) + [pl.BlockSpec((tm, C), lambda i, offs: (i, 0)),
                                    pl.BlockSpec(memory_space=pl.ANY)],
        out_specs=pl.BlockSpec(memory_space=pl.ANY),
        scratch_shapes=[pltpu.SemaphoreType.DMA(())],
    )
    return pl.pallas_call(
        functools.partial(_dispatch_kernel, tm=tm), grid_spec=grid_spec,
        out_shape=jax.ShapeDtypeStruct(xs0.shape, xs0.dtype),
        input_output_aliases={4: 0},
        compiler_params=_cparams(("arbitrary",)), name="moe_dispatch",
    )(offs, eidx, ridx, hp, xs0)


def _expert_kernel(te_ref, nv_ref, xs_ref, wg_ref, wu_ref, wd_ref, ys_ref):
    del te_ref
    t = pl.program_id(0)

    @pl.when(t < nv_ref[0])
    def _():
        lo, hi = _unpack_halves(xs_ref[...])
        lo = lo.astype(BF16)
        hi = hi.astype(BF16)
        half = D_MODEL // 2
        a = (jnp.dot(lo, wg_ref[0:half, :], preferred_element_type=F32)
             + jnp.dot(hi, wg_ref[half:D_MODEL, :], preferred_element_type=F32))
        b = (jnp.dot(lo, wu_ref[0:half, :], preferred_element_type=F32)
             + jnp.dot(hi, wu_ref[half:D_MODEL, :], preferred_element_type=F32))
        hh = ((a * jax.nn.sigmoid(a)) * b).astype(BF16)
        ys_ref[...] = _pack_halves(jnp.dot(hh, wd_ref[...], preferred_element_type=F32))

    @pl.when(t >= nv_ref[0])
    def _():
        ys_ref[...] = jnp.zeros(ys_ref.shape, ys_ref.dtype)


def _experts(tile_expert, n_valid, xs, wg, wu, wd, layer):
    NP, C = xs.shape
    wspec = lambda shape: pl.BlockSpec((None, None) + shape, lambda t, te, nv: (layer, te[t], 0, 0))
    grid_spec = pltpu.PrefetchScalarGridSpec(
        num_scalar_prefetch=2, grid=(NP // MOE_TM,),
        in_specs=[pl.BlockSpec((MOE_TM, C), lambda t, te, nv: (t, 0)),
                  wspec((D_MODEL, D_EXPERT)), wspec((D_MODEL, D_EXPERT)), wspec((D_EXPERT, D_MODEL))],
        out_specs=pl.BlockSpec((MOE_TM, C), lambda t, te, nv: (t, 0)),
    )
    return pl.pallas_call(
        _expert_kernel, grid_spec=grid_spec, out_shape=jax.ShapeDtypeStruct(xs.shape, xs.dtype),
        compiler_params=_cparams(("arbitrary",)), name="moe_experts",
    )(tile_expert, n_valid, xs, wg, wu, wd)


def _combine_kernel(offs_ref, e_ref, rk_ref, h_ref, route_ref, ys_ref, y_ref, r1_ref, r2_ref, sem,
                    *, tm):
    def start(r, c):
        _row_copy(ys_ref, _slot_row(offs_ref, e_ref, rk_ref, r), r1_ref, r, sem).start()
        _row_copy(ys_ref, _slot_row(offs_ref, e_ref, rk_ref, tm + r), r2_ref, r, sem).start()
        return c

    lax.fori_loop(0, tm, start, 0, unroll=DMA_UNROLL)
    _wait_rows(ys_ref, r1_ref, sem, 2 * tm)
    route = route_ref[...]
    w1 = route[:, 4:5]
    w2 = route[:, 5:6]
    a_lo, a_hi = _unpack_halves(r1_ref[...])
    b_lo, b_hi = _unpack_halves(r2_ref[...])
    half = D_MODEL // 2
    y_ref[:, 0:half] = h_ref[:, 0:half] + w1 * a_lo + w2 * b_lo
    y_ref[:, half:D_MODEL] = h_ref[:, half:D_MODEL] + w1 * a_hi + w2 * b_hi


def _combine(offs, eidx, ridx, h, route, ys, tm):
    T = h.shape[0]
    C = ys.shape[1]
    row = lambda i, offs: (i, 0)
    grid_spec = pltpu.PrefetchScalarGridSpec(
        num_scalar_prefetch=1, grid=(T // tm,),
        in_specs=_slot_specs(tm) + [pl.BlockSpec((tm, D_MODEL), row), pl.BlockSpec((tm, LANES), row),
                                    pl.BlockSpec(memory_space=pl.ANY)],
        out_specs=pl.BlockSpec((tm, D_MODEL), row),
        scratch_shapes=[pltpu.VMEM((tm, C), jnp.uint32), pltpu.VMEM((tm, C), jnp.uint32),
                        pltpu.SemaphoreType.DMA(())],
    )
    return pl.pallas_call(
        functools.partial(_combine_kernel, tm=tm), grid_spec=grid_spec,
        out_shape=jax.ShapeDtypeStruct((T, D_MODEL), F32),
        compiler_params=_cparams(("arbitrary",)), name="moe_combine",
    )(offs, eidx, ridx, h, route, ys)


def _moe_sparse(h, hp, route, route_t, counts, wg, wu, wd, layer, tm):
    T = h.shape[0]
    n_pad = 2 * T + N_EXPERTS * MOE_TM
    cnt = counts[0, ROUTER_OFF:ROUTER_OFF + N_EXPERTS].astype(jnp.int32)
    padded = ((cnt + MOE_TM - 1) // MOE_TM) * MOE_TM
    ends = jnp.cumsum(padded)
    offs = ends - padded
    tile_start = jnp.arange(n_pad // MOE_TM, dtype=jnp.int32) * MOE_TM
    tile_expert = jnp.minimum(jnp.sum(tile_start[:, None] >= ends[None, :], axis=1),
                              N_EXPERTS - 1).astype(jnp.int32)
    n_valid = (ends[-1] // MOE_TM).astype(jnp.int32).reshape(1)
    per_step = lambda rows: jnp.transpose(rows.astype(jnp.int32).reshape(2, T // tm, tm),
                                          (1, 0, 2)).reshape(T // tm, 1, 2 * tm)
    eidx = per_step(route_t[0:2])
    ridx = per_step(route_t[2:4])
    offs = offs.astype(jnp.int32)
    xs0 = jnp.zeros((n_pad, D_MODEL // 2), jnp.uint32)
    xs = _dispatch(offs, eidx, ridx, hp, xs0, tm)
    ys = _experts(tile_expert, n_valid, xs, wg, wu, wd, layer)
    return _combine(offs, eidx, ridx, h, route, ys, tm)


def _pick(n, cands):
    for c in cands:
        if n % c == 0:
            return c
    raise ValueError(f"no tile for {n}")


def _layer(x, l, depth, P, w, kv_all, cache_k, cache_v, pool_prev, B, L, is_prompt):
    T = B * L
    lam_init = 0.8 - 0.6 * math.exp(-0.3 * l)
    tm = _pick(T, (1024, 512, 256))
    q, kf, kb, vf, vb, bc, xn = _inproj(x, P["norm1"], w["w_in"], w["e_mat"], P["gq"], P["gk"],
                                        kv_all, l, depth, tm)
    gt = _gates(xn, w["w_gates"], tm)

    if is_prompt:
        tq = _pick(L, (2048, 1024, 512, 256, 128, 64))
        oa = _attn_prompt(q, kb, vb, P["lam"], P["g_sub"], P["attn_fast"], B, L, tq, lam_init)
    else:
        oa = _attn_sample(q, kb, vb, cache_k, cache_v, l, P["lam"], P["g_sub"], B, L, lam_init)

    xc = bc[:, 2 * W_B:].reshape(B, L, W_C)
    if is_prompt:
        tmb = _pick(L, (512, 256, 128))
        csz = SG_CHUNK
        nt = L // tmb
        tails = xc.reshape(B, nt, tmb, W_C)[:, :, tmb - HALO:, :]
        halo = jnp.concatenate([jnp.zeros((B, 1, HALO, W_C), F32), tails[:, :-1]], axis=1)
        halo = halo.reshape(B * nt, HALO, W_C)
        pos0 = 0
        pool_new = xc[:, L - POOL_STATE:, :]
    else:
        tmb = L
        csz = L
        halo = jnp.concatenate([jnp.zeros((B, HALO - POOL_STATE, W_C), F32), pool_prev], axis=1)
        pos0 = cache_k.shape[2]
        pool_new = jnp.concatenate([pool_prev, xc], axis=1)[:, -POOL_STATE:, :]
    ob, oc, vbn = _branches(bc, halo, P["g_vb"], w["w_sp"][:, :csz, :csz], w["b_sp_full"][:csz],
                            w["w_pool"], P["pool_scale"], tmb, csz, L, pos0)

    tm2 = _pick(T, (MERGE_TM,))
    h, hp, route, route_t, counts = _merge(x, oa, ob, oc, gt, w["w_branch"], w["w_out"], P["norm2"],
                                           w["wr_hi"], w["wr_lo"], P["b_r"], w["tri_rank"], tm2)
    y = _moe_sparse(h, hp, route, route_t, counts, w["w_e_gate"], w["w_e_up"], w["w_e_down"], l, tm2)
    return y, (kf, vf), pool_new, vbn


def kernel(x_prompt, x_sample, cache_k, cache_v, state_pool, norm1, w_in, g_q, g_k, lam, g_sub,
           g_vb, w_sp, b_sp, w_pool, pool_scale, w_branch, w_out, norm2, w_rg, b_rg, w_re, b_re,
           w_e_gate, w_e_up, w_e_down):
    depth = w_in.shape[0]
    B, S, D = x_prompt.shape
    Bs, Ls, _ = x_sample.shape
    PL = cache_k.shape[2]
    ck = cache_k.reshape(depth, Bs, PL, W_A)
    cv = cache_v.reshape(depth, Bs, PL, W_A)

    rep = IN_TN // HD_A
    gidx = jnp.arange(IN_TN) // HD_A
    e_mat = (gidx[:, None] == gidx[None, :]).astype(BF16)
    tri = jnp.tril(jnp.ones((SG_CHUNK, SG_CHUNK), bool))
    weg, weu, wed = w_e_gate.astype(BF16), w_e_up.astype(BF16), w_e_down.astype(BF16)
    ridx = jnp.arange(MERGE_TM)
    tri_rank = (ridx[None, :] < ridx[:, None]).astype(BF16)

    xp = x_prompt.reshape(B * S, D)
    xs = x_sample.reshape(Bs * Ls, D)
    outs = {k: [] for k in ("pp", "ps", "gs")}
    kv_p = (jnp.zeros((depth, B * S, W_A), F32), jnp.zeros((depth, B * S, W_A), F32))
    kv_s = (jnp.zeros((depth, Bs * Ls, W_A), F32), jnp.zeros((depth, Bs * Ls, W_A), F32))
    for l in range(depth):
        w_r = jnp.zeros((D, LANES), F32)
        w_r = w_r.at[:, :N_EXPERT_GROUPS].set(w_rg[l]).at[:, ROUTER_OFF:ROUTER_OFF + N_EXPERTS].set(w_re[l])
        wr_hi = w_r.astype(BF16)
        wr_lo = (w_r - wr_hi.astype(F32)).astype(BF16)
        b_r = jnp.zeros((1, LANES), F32)
        b_r = b_r.at[0, :N_EXPERT_GROUPS].set(b_rg[l]).at[0, ROUTER_OFF:ROUTER_OFF + N_EXPERTS].set(b_re[l])
        score_bound = HD_A * QK_SCALE * jnp.max(jnp.abs(g_q[l])) * jnp.max(jnp.abs(g_k[l]))
        attn_fast = (score_bound <= MAX_UNSHIFTED_SCORE).astype(jnp.int32).reshape(1)
        P = dict(norm1=norm1[l][None], norm2=norm2[l][None], attn_fast=attn_fast,
                 gq=(jnp.tile(g_q[l], rep) * QK_SCALE)[None], gk=jnp.tile(g_k[l], rep)[None],
                 lam=lam[l], g_sub=g_sub[l][None], g_vb=g_vb[l][None],
                 pool_scale=pool_scale[l][None], b_r=b_r)
        w = dict(w_in=w_in[l, :, :GATE_COL0].astype(BF16), w_gates=w_in[l, :, GATE_COL0:].astype(BF16),
                 e_mat=e_mat,
                 w_sp=jnp.where(tri, w_sp[l], 0.0).astype(BF16),
                 b_sp_full=jnp.repeat(jnp.transpose(b_sp[l]), LANES, axis=1),
                 w_pool=w_pool[l].astype(BF16), w_branch=w_branch[l].astype(BF16),
                 w_out=w_out[l].astype(BF16), wr_hi=wr_hi, wr_lo=wr_lo,
                 w_e_gate=weg, w_e_up=weu, w_e_down=wed, tri_rank=tri_rank)
        xp, kv_p, pool1, _ = _layer(xp, l, depth, P, w, kv_p, None, None, None, B, S, True)
        xs, kv_s, pool2, vb2 = _layer(xs, l, depth, P, w, kv_s, ck, cv, state_pool[l], Bs, Ls, False)
        outs["pp"].append(pool1)
        outs["ps"].append(pool2)
        outs["gs"].append(vb2.reshape(Bs, Ls, W_B))
    st = lambda k: jnp.stack(outs[k])
    return (xp.reshape(B, S, D), xs.reshape(Bs, Ls, D),
            kv_p[0].reshape(depth, B, S, N_HEADS_A, 2, HD_A),
            kv_p[1].reshape(depth, B, S, N_HEADS_A, 2 * HD_A), st("pp"),
            kv_s[0].reshape(depth, Bs, Ls, N_HEADS_A, 2, HD_A),
            kv_s[1].reshape(depth, Bs, Ls, N_HEADS_A, 2 * HD_A), st("ps"), st("gs"))
```

```python
import functools
import math

import jax
import jax.numpy as jnp
from jax import lax
from jax.experimental import pallas as pl
from jax.experimental.pallas import tpu as pltpu

F32 = jnp.float32
BF16 = jnp.bfloat16

D_MODEL = 2048
CHUNK = 64
N_HEADS_A = 8
HD_A = 64
W_A = N_HEADS_A * 2 * HD_A
SG_CHUNK = 128
N_GROUPS_B = 4
W_B = 512
POOL_WINDOWS = (2, 4, 8, 16)
N_GROUPS_C = 4
W_C = 512
POOL_STATE = 15
N_BRANCH = 3
IN_COLS = W_A * 3 + W_B * 2 + W_C + N_BRANCH * D_MODEL
N_EXPERT_GROUPS = 4
EXPERTS_PER_GROUP = 8
N_EXPERTS = N_EXPERT_GROUPS * EXPERTS_PER_GROUP
D_EXPERT = 256
EPS = 1e-6

LANES = 128
HALO = 16
NEG = -1e30
ATTN_RBLK = 512
QK_SCALE = (HD_A ** -0.5) * math.log2(math.e)
MAX_UNSHIFTED_SCORE = 48.0
VMEM_LIMIT = 58 * 1024 * 1024
ROUTER_OFF = N_EXPERT_GROUPS

IN_TN = 512
_JQ, _JK, _JV, _JBC, _JG = 0, 2, 4, 6, 9
GATE_COL0 = _JG * IN_TN
GATES_TN = 1024
GATES_RBLK = 256
INPROJ_RBLK = 512
MERGE_TM = 256
MERGE_RBLK = 256
MOE_TM = 256
ROUTE_ROWS = 8
DMA_UNROLL = 8


def _cparams(sem):
    return pltpu.CompilerParams(dimension_semantics=sem, vmem_limit_bytes=VMEM_LIMIT)


def _inproj_kernel(x_ref, g1_ref, w_ref, e_ref, gq_ref, gk_ref, wkt_ref, gkt_ref, k_all_in, v_all_in,
                   q_ref, kf_ref, kb_ref, vf_ref, vb_ref, bc_ref, xn_ref, *, tm, rblk, k_transposed):
    del k_all_in, v_all_in
    j = pl.program_id(1)

    @pl.when(j == 0)
    def _():
        x = x_ref[...]
        ms = jnp.mean(x * x, axis=-1, keepdims=True)
        xn_ref[...] = (x * lax.rsqrt(ms + EPS) * g1_ref[...]).astype(BF16)

    def headnorm(p, g):
        ss = jnp.dot((p * p).astype(BF16), e_ref[...], preferred_element_type=F32)
        return p * lax.rsqrt(ss * (1.0 / HD_A) + EPS) * g

    def row_blocks(epilogue):
        def run():
            for r0 in range(0, tm, rblk):
                rows = slice(r0, r0 + rblk)
                epilogue(rows, jnp.dot(xn_ref[rows, :], w_ref[...], preferred_element_type=F32))
        return run

    def q_out(rows, p):
        q_ref[rows, :] = headnorm(p, gq_ref[...]).astype(BF16)

    def k_out(rows, p):
        kn = headnorm(p, gk_ref[...])
        kf_ref[rows, :] = kn
        kb_ref[rows, :] = kn.astype(BF16)

    def v_out(rows, p):
        vf_ref[rows, :] = p
        vb_ref[rows, :] = p.astype(BF16)

    def bc_out(rows, p):
        bc_ref[rows, :] = p

    def k_out_transposed():
        heads = IN_TN // HD_A
        for r0 in range(0, tm, rblk):
            rows = slice(r0, r0 + rblk)
            pt = lax.dot_general(wkt_ref[...], xn_ref[rows, :], (((1,), (1,)), ((), ())),
                                 preferred_element_type=F32)
            p3 = pt.reshape(heads, HD_A, rblk)
            ss = jnp.sum(p3 * p3, axis=1, keepdims=True)
            kn = (p3 * lax.rsqrt(ss * (1.0 / HD_A) + EPS)).reshape(IN_TN, rblk)
            kn = kn * jnp.tile(gkt_ref[...], (1, rblk // LANES))
            kf_ref[:, rows] = kn
            kb_ref[:, rows] = kn.astype(BF16)

    pl.when(j < _JK)(row_blocks(q_out))
    pl.when((j >= _JK) & (j < _JV))(k_out_transposed if k_transposed else row_blocks(k_out))
    pl.when((j >= _JV) & (j < _JBC))(row_blocks(v_out))
    pl.when(j >= _JBC)(row_blocks(bc_out))


def _gates_kernel(xn_ref, w_ref, gt_ref, *, tm, rblk):
    for r0 in range(0, tm, rblk):
        p = jnp.dot(xn_ref[r0:r0 + rblk, :], w_ref[...], preferred_element_type=F32)
        gt_ref[r0:r0 + rblk, :] = (0.5 * jnp.tanh(0.5 * p) + 0.5).astype(BF16)


def _gates(xn, w_g, tm):
    T = xn.shape[0]
    ncol = w_g.shape[1]
    return pl.pallas_call(
        functools.partial(_gates_kernel, tm=tm, rblk=min(GATES_RBLK, tm)),
        grid=(T // tm, ncol // GATES_TN),
        in_specs=[pl.BlockSpec((tm, D_MODEL), lambda i, j: (i, 0)),
                  pl.BlockSpec((D_MODEL, GATES_TN), lambda i, j: (0, j))],
        out_specs=pl.BlockSpec((tm, GATES_TN), lambda i, j: (i, j)),
        out_shape=jax.ShapeDtypeStruct((T, ncol), BF16),
        compiler_params=_cparams(("arbitrary", "arbitrary")), name="gates",
    )(xn, w_g)


def _inproj(x, g1, w_bf, e_mat, gq, gk, wkt, gkt, kv_all, layer, depth, tm, seq_k_transposed):
    T = x.shape[0]
    grid = (T // tm, _JG)

    def seg(j0, n):
        return lambda i, j: (i, jnp.clip(j - j0, 0, n - 1))

    def seg_l(j0, n):
        return lambda i, j: (layer, i, jnp.clip(j - j0, 0, n - 1))

    blk = lambda f: pl.BlockSpec((tm, IN_TN), f)
    blk_l = lambda f: pl.BlockSpec((None, tm, IN_TN), f)
    if seq_k_transposed is None:
        kf_shape, kb_shape = (depth, T, W_A), (T, W_A)
        kf_spec, kb_spec = blk_l(seg_l(_JK, 2)), blk(seg(_JK, 2))
    else:
        S = seq_k_transposed
        nsb = S // tm
        kf_shape, kb_shape = (depth, T // S, W_A, S), (T // S, W_A, S)
        kf_spec = pl.BlockSpec((None, None, IN_TN, tm),
                               lambda i, j: (layer, i // nsb, jnp.clip(j - _JK, 0, 1), i % nsb))
        kb_spec = pl.BlockSpec((None, IN_TN, tm),
                               lambda i, j: (i // nsb, jnp.clip(j - _JK, 0, 1), i % nsb))
    out_shape = (
        jax.ShapeDtypeStruct((T, W_A), BF16),
        jax.ShapeDtypeStruct(kf_shape, F32),
        jax.ShapeDtypeStruct(kb_shape, BF16),
        jax.ShapeDtypeStruct((depth, T, W_A), F32),
        jax.ShapeDtypeStruct((T, W_A), BF16),
        jax.ShapeDtypeStruct((T, 2 * W_B + W_C), F32),
        jax.ShapeDtypeStruct((T, D_MODEL), BF16),
    )
    out_specs = (blk(seg(_JQ, 2)), kf_spec, kb_spec, blk_l(seg_l(_JV, 2)),
                 blk(seg(_JV, 2)), blk(seg(_JBC, 3)),
                 pl.BlockSpec((tm, D_MODEL), lambda i, j: (i, 0)))
    in_specs = [
        pl.BlockSpec((tm, D_MODEL), lambda i, j: (i, 0)),
        pl.BlockSpec((1, D_MODEL), lambda i, j: (0, 0)),
        pl.BlockSpec((D_MODEL, IN_TN), lambda i, j: (0, j)),
        pl.BlockSpec((IN_TN, IN_TN), lambda i, j: (0, 0)),
        pl.BlockSpec((1, IN_TN), lambda i, j: (0, 0)),
        pl.BlockSpec((1, IN_TN), lambda i, j: (0, 0)),
        pl.BlockSpec((IN_TN, D_MODEL), lambda i, j: (jnp.clip(j - _JK, 0, 1), 0)),
        pl.BlockSpec((IN_TN, LANES), lambda i, j: (0, 0)),
    ]
    in_specs += [pl.BlockSpec(memory_space=pl.ANY)] * 2
    return pl.pallas_call(
        functools.partial(_inproj_kernel, tm=tm, rblk=min(INPROJ_RBLK, tm),
                          k_transposed=seq_k_transposed is not None),
        grid=grid, in_specs=in_specs, out_specs=out_specs, out_shape=out_shape,
        input_output_aliases={8: 1, 9: 3},
        compiler_params=_cparams(("arbitrary", "arbitrary")), name="inproj",
    )(x, g1, w_bf, e_mat, gq, gk, wkt, gkt, *kv_all)


def _lambda_value(lam_ref, lam_init):
    lp = lam_ref[...]
    a = jnp.sum(lp[0:1] * lp[1:2], axis=-1, keepdims=True)
    b = jnp.sum(lp[2:3] * lp[3:4], axis=-1, keepdims=True)
    return jnp.exp(a) - jnp.exp(b) + lam_init


def _stack_q(q, qs_ref, tq):
    lane = lax.broadcasted_iota(jnp.int32, q.shape, 1)
    zero = jnp.zeros_like(q)
    qs_ref[0:tq, :] = jnp.where(lane < HD_A, q, zero)
    qs_ref[tq:2 * tq, :] = jnp.where(lane >= HD_A, q, zero)


def _attn_finish(acc, l, lam, gsub, tq, lam_init):
    o = acc / l
    o = o[0:tq] - lam * o[tq:2 * tq]
    ms = jnp.mean(o * o, axis=-1, keepdims=True)
    return o * lax.rsqrt(ms + EPS) * gsub * (1.0 - lam_init)


def _attn_prompt_kernel(qi_ref, kj_ref, fast_ref, q_ref, k_ref, v_ref, lam_ref, gsub_ref, o_ref,
                        qs_ref, m_ref, l_ref, acc_ref, *, tq, rblk, lam_init):
    t = pl.program_id(2)
    qi = qi_ref[t]
    kj = kj_ref[t]
    fast = fast_ref[0] != 0

    @pl.when(kj == 0)
    def _():
        _stack_q(q_ref[...], qs_ref, tq)
        m_ref[...] = jnp.full(m_ref.shape, NEG, F32)
        l_ref[...] = jnp.zeros(l_ref.shape, F32)
        acc_ref[...] = jnp.zeros(acc_ref.shape, F32)

    def step(masked, bounded):
        for rb in range(2 * tq // rblk):
            r0 = rb * rblk
            q0 = r0 % tq
            nk = min(tq, q0 + rblk) if masked else tq
            s = jnp.dot(qs_ref[r0:r0 + rblk, :], k_ref[:, 0:nk],
                        preferred_element_type=F32)
            if masked:
                r = lax.broadcasted_iota(jnp.int32, s.shape, 0) + q0
                c = lax.broadcasted_iota(jnp.int32, s.shape, 1)
                s = jnp.where((c // CHUNK) <= (r // CHUNK), s, NEG)
            if bounded:
                p = jnp.exp2(s)
                l_ref[r0:r0 + rblk, :] += jnp.sum(p, axis=-1, keepdims=True)
                acc_ref[r0:r0 + rblk, :] += jnp.dot(p.astype(BF16), v_ref[0:nk, :],
                                                    preferred_element_type=F32)
                continue
            m_prev = m_ref[r0:r0 + rblk, :]
            m_new = jnp.maximum(m_prev, jnp.max(s, axis=-1, keepdims=True))
            alpha = jnp.exp2(m_prev - m_new)
            p = jnp.exp2(s - jnp.tile(m_new, (1, nk // LANES)))
            l_ref[r0:r0 + rblk, :] = alpha * l_ref[r0:r0 + rblk, :] + jnp.sum(p, axis=-1, keepdims=True)
            acc_ref[r0:r0 + rblk, :] = alpha * acc_ref[r0:r0 + rblk, :] + jnp.dot(
                p.astype(BF16), v_ref[0:nk, :], preferred_element_type=F32)
            m_ref[r0:r0 + rblk, :] = m_new

    for bounded in (True, False):
        use = fast if bounded else jnp.logical_not(fast)
        pl.when((kj < qi) & use)(functools.partial(step, False, bounded))
        pl.when((kj == qi) & use)(functools.partial(step, True, bounded))

    @pl.when(kj == qi)
    def _():
        lam = _lambda_value(lam_ref, lam_init)
        o_ref[...] = _attn_finish(acc_ref[...], l_ref[...], lam, gsub_ref[...], tq,
                                  lam_init).astype(o_ref.dtype)


def _attn_prompt(q, k, v, lam_p, gsub, fast, B, S, tq, lam_init):
    nq = S // tq
    pairs = [(i, j) for i in range(nq) for j in range(i + 1)]
    qi = jnp.asarray([p[0] for p in pairs], jnp.int32)
    kj = jnp.asarray([p[1] for p in pairs], jnp.int32)
    grid_spec = pltpu.PrefetchScalarGridSpec(
        num_scalar_prefetch=3,
        grid=(B, N_HEADS_A, len(pairs)),
        in_specs=[
            pl.BlockSpec((tq, LANES), lambda b, h, t, qi, kj, f: (b * nq + qi[t], h)),
            pl.BlockSpec((None, LANES, tq), lambda b, h, t, qi, kj, f: (b, h, kj[t])),
            pl.BlockSpec((tq, LANES), lambda b, h, t, qi, kj, f: (b * nq + kj[t], h)),
            pl.BlockSpec((4, HD_A), lambda b, h, t, qi, kj, f: (0, 0)),
            pl.BlockSpec((1, LANES), lambda b, h, t, qi, kj, f: (0, 0)),
        ],
        out_specs=pl.BlockSpec((tq, LANES), lambda b, h, t, qi, kj, f: (b * nq + qi[t], h)),
        scratch_shapes=[pltpu.VMEM((2 * tq, LANES), BF16), pltpu.VMEM((2 * tq, LANES), F32),
                        pltpu.VMEM((2 * tq, LANES), F32), pltpu.VMEM((2 * tq, LANES), F32)],
    )
    return pl.pallas_call(
        functools.partial(_attn_prompt_kernel, tq=tq, rblk=min(ATTN_RBLK, tq), lam_init=lam_init),
        grid_spec=grid_spec, out_shape=jax.ShapeDtypeStruct((B * S, W_A), BF16),
        compiler_params=_cparams(("arbitrary", "arbitrary", "arbitrary")), name="attn_prompt",
    )(qi, kj, fast, q, k, v, lam_p, gsub)


def _attn_sample_kernel(q_ref, kc_ref, vc_ref, kn_ref, vn_ref, lam_ref, gsub_ref, o_ref,
                        qs_ref, *, tq, lam_init):
    _stack_q(q_ref[...], qs_ref, tq)
    qs = qs_ref[...]
    dn = (((1,), (1,)), ((), ()))
    s_c = jnp.dot(qs, kc_ref[...].astype(BF16), preferred_element_type=F32)
    s_n = lax.dot_general(qs, kn_ref[...], dn, preferred_element_type=F32)
    m = jnp.maximum(jnp.max(s_c, axis=-1, keepdims=True), jnp.max(s_n, axis=-1, keepdims=True))
    p_c = jnp.exp2(s_c - m)
    p_n = jnp.exp2(s_n - m)
    l = jnp.sum(p_c, axis=-1, keepdims=True) + jnp.sum(p_n, axis=-1, keepdims=True)
    acc = (jnp.dot(p_c.astype(BF16), vc_ref[...].astype(BF16), preferred_element_type=F32)
           + jnp.dot(p_n.astype(BF16), vn_ref[...], preferred_element_type=F32))
    lam = _lambda_value(lam_ref, lam_init)
    o_ref[...] = _attn_finish(acc, l, lam, gsub_ref[...], tq, lam_init).astype(o_ref.dtype)


def _attn_sample(q, kn, vn, cache_k, cache_v, layer, lam_p, gsub, B, L, lam_init):
    P = cache_v.shape[2]
    cspec = pl.BlockSpec((None, None, P, LANES), lambda b, h: (layer, b, 0, h))
    kspec = pl.BlockSpec((None, None, LANES, P), lambda b, h: (layer, b, h, 0))
    nspec = pl.BlockSpec((L, LANES), lambda b, h: (b, h))
    return pl.pallas_call(
        functools.partial(_attn_sample_kernel, tq=L, lam_init=lam_init),
        grid=(B, N_HEADS_A),
        in_specs=[nspec, kspec, cspec, nspec, nspec,
                  pl.BlockSpec((4, HD_A), lambda b, h: (0, 0)),
                  pl.BlockSpec((1, LANES), lambda b, h: (0, 0))],
        out_specs=nspec,
        out_shape=jax.ShapeDtypeStruct((B * L, W_A), BF16),
        scratch_shapes=[pltpu.VMEM((2 * L, LANES), BF16)],
        compiler_params=_cparams(("arbitrary", "arbitrary")), name="attn_sample",
    )(q, cache_k, cache_v, kn, vn, lam_p, gsub)


def _branches_kernel(bc_ref, halo_ref, gvb_ref, wsp_ref, bsp_ref, wpool_ref, pscale_ref,
                     ob_ref, oc_ref, vbn_ref, *, tm, csz, seq, pos0):
    i = pl.program_id(0)
    u = bc_ref[:, 0:W_B]
    vb = bc_ref[:, W_B:2 * W_B]
    xc = bc_ref[:, 2 * W_B:2 * W_B + W_C]

    ms = jnp.mean(vb * vb, axis=-1, keepdims=True)
    vbn = vb * lax.rsqrt(ms + EPS) * gvb_ref[...]
    vbn_ref[...] = vbn
    vbn_bf = vbn.astype(BF16)
    for c in range(tm // csz):
        r0 = c * csz
        for g in range(N_GROUPS_B):
            c0 = g * LANES
            z = jnp.dot(wsp_ref[g], vbn_bf[r0:r0 + csz, c0:c0 + LANES],
                        preferred_element_type=F32) + bsp_ref[:, c0:c0 + LANES]
            ob_ref[r0:r0 + csz, c0:c0 + LANES] = (u[r0:r0 + csz, c0:c0 + LANES] * z).astype(BF16)

    ext = jnp.concatenate([halo_ref[0], xc], axis=0)
    row = lax.broadcasted_iota(jnp.int32, (tm, 1), 0)
    pos = (pos0 + (i * tm) % seq + row).astype(F32)
    acc = ext
    shift = 1
    for g, w in enumerate(POOL_WINDOWS):
        while shift < w:
            acc = acc + pltpu.roll(acc, shift, axis=0)
            shift *= 2
        c0 = g * LANES
        cnt = jnp.minimum(float(w), pos + 1.0)
        pooled = acc[HALO:HALO + tm, c0:c0 + LANES] / cnt
        diff = (pooled - xc[:, c0:c0 + LANES]).astype(BF16)
        h = jnp.dot(diff, wpool_ref[g], preferred_element_type=F32)
        oc_ref[:, c0:c0 + LANES] = (h * pscale_ref[:, c0:c0 + LANES]).astype(BF16)


def _branches(bc, halo, gvb, wsp, bsp_full, wpool, pscale, tm, csz, seq, pos0):
    T = bc.shape[0]
    row = lambda i: (i, 0)
    full2 = lambda i: (0, 0)
    full3 = lambda i: (0, 0, 0)
    return pl.pallas_call(
        functools.partial(_branches_kernel, tm=tm, csz=csz, seq=seq, pos0=pos0),
        grid=(T // tm,),
        in_specs=[pl.BlockSpec((tm, 2 * W_B + W_C), row),
                  pl.BlockSpec((1, HALO, W_C), lambda i: (i, 0, 0)),
                  pl.BlockSpec((1, W_B), full2),
                  pl.BlockSpec((N_GROUPS_B, csz, csz), full3),
                  pl.BlockSpec((csz, W_B), full2),
                  pl.BlockSpec((N_GROUPS_C, LANES, LANES), full3),
                  pl.BlockSpec((1, W_C), full2)],
        out_specs=(pl.BlockSpec((tm, W_B), row), pl.BlockSpec((tm, W_C), row),
                   pl.BlockSpec((tm, W_B), row)),
        out_shape=(jax.ShapeDtypeStruct((T, W_B), BF16), jax.ShapeDtypeStruct((T, W_C), BF16),
                   jax.ShapeDtypeStruct((T, W_B), F32)),
        compiler_params=_cparams(("arbitrary",)), name="branches",
    )(bc, halo, gvb, wsp, bsp_full, wpool, pscale)


def _pack_halves(x):
    c = x.shape[1] // 2
    lo = lax.bitcast_convert_type(x[:, :c].astype(BF16).astype(F32), jnp.uint32)
    hi = lax.bitcast_convert_type(x[:, c:].astype(BF16).astype(F32), jnp.uint32)
    return (lo >> 16) | (hi & jnp.uint32(0xFFFF0000))


def _unpack_halves(u):
    lo = lax.bitcast_convert_type(u << 16, F32)
    hi = lax.bitcast_convert_type(u & jnp.uint32(0xFFFF0000), F32)
    return lo, hi


def _merge_kernel(x_ref, oa_ref, ob_ref, oc_ref, gt_ref, wb_ref, wo_ref, g2_ref,
                  wrh_ref, wrl_ref, br_ref, tri_ref, h_ref, hp_ref, route_ref, route_t_ref, cnt_ref,
                  carry_ref, *, tm, rblk):
    @pl.when(pl.program_id(0) == 0)
    def _():
        carry_ref[...] = jnp.zeros(carry_ref.shape, F32)

    for r0 in range(0, tm, rblk):
        _merge_rows(slice(r0, r0 + rblk), rblk, x_ref, oa_ref, ob_ref, oc_ref, gt_ref, wb_ref, wo_ref,
                    g2_ref, wrh_ref, wrl_ref, br_ref, tri_ref, h_ref, hp_ref, route_ref, route_t_ref,
                    cnt_ref, carry_ref)


def _merge_rows(rows, rblk, x_ref, oa_ref, ob_ref, oc_ref, gt_ref, wb_ref, wo_ref, g2_ref,
                wrh_ref, wrl_ref, br_ref, tri_ref, h_ref, hp_ref, route_ref, route_t_ref, cnt_ref,
                carry_ref):
    ta = jnp.dot(oa_ref[rows, :], wb_ref[0:W_A, :], preferred_element_type=F32)
    tb = jnp.dot(ob_ref[rows, :], wb_ref[W_A:W_A + W_B, :], preferred_element_type=F32)
    tc = jnp.dot(oc_ref[rows, :], wb_ref[W_A + W_B:W_A + W_B + W_C, :], preferred_element_type=F32)
    merged = (gt_ref[rows, 0:D_MODEL].astype(F32) * ta
              + gt_ref[rows, D_MODEL:2 * D_MODEL].astype(F32) * tb
              + gt_ref[rows, 2 * D_MODEL:3 * D_MODEL].astype(F32) * tc)
    h = x_ref[rows, :] + jnp.dot(merged.astype(BF16), wo_ref[...], preferred_element_type=F32)
    h_ref[rows, :] = h
    ms = jnp.mean(h * h, axis=-1, keepdims=True)
    hn = h * lax.rsqrt(ms + EPS) * g2_ref[...]
    hp_ref[rows, :] = _pack_halves(hn)

    hi = hn.astype(BF16)
    lo = (hn - hi.astype(F32)).astype(BF16)
    lg = (jnp.dot(hi, wrh_ref[...], preferred_element_type=F32)
          + jnp.dot(lo, wrh_ref[...], preferred_element_type=F32)
          + jnp.dot(hi, wrl_ref[...], preferred_element_type=F32)) + br_ref[...]
    lane = lax.broadcasted_iota(jnp.int32, lg.shape, 1).astype(F32)
    big = float(LANES)
    gmask = lane < N_EXPERT_GROUPS
    lgg = jnp.where(gmask, lg, NEG)
    mg = jnp.max(lgg, axis=-1, keepdims=True)
    sg = jnp.sum(jnp.where(gmask, jnp.exp(lgg - mg), 0.0), axis=-1, keepdims=True)
    g_p = 1.0 / sg
    g_sel = jnp.min(jnp.where(lgg == mg, lane, big), axis=-1, keepdims=True)
    e0 = ROUTER_OFF + g_sel * EXPERTS_PER_GROUP
    emask = (lane >= e0) & (lane < e0 + EXPERTS_PER_GROUP)
    le = jnp.where(emask, lg, NEG)
    me = jnp.max(le, axis=-1, keepdims=True)
    ee = jnp.where(emask, jnp.exp(le - me), 0.0)
    pe = ee / jnp.sum(ee, axis=-1, keepdims=True)
    pe = jnp.where(emask, pe, -1.0)
    top1 = jnp.max(pe, axis=-1, keepdims=True)
    i1 = jnp.min(jnp.where(pe == top1, lane, big), axis=-1, keepdims=True)
    pe2 = jnp.where(lane == i1, -1.0, pe)
    top2 = jnp.max(pe2, axis=-1, keepdims=True)
    i2 = jnp.min(jnp.where(pe2 == top2, lane, big), axis=-1, keepdims=True)
    den = top1 + top2
    w1 = g_p * (top1 / den)
    w2 = g_p * (top2 / den)

    oh1 = lane == i1
    oh2 = lane == i2
    oh = jnp.where(oh1 | oh2, 1.0, 0.0)
    prefix = carry_ref[...] + jnp.dot(tri_ref[0:rblk, 0:rblk], oh.astype(BF16),
                                      preferred_element_type=F32)
    rank1 = jnp.sum(jnp.where(oh1, prefix, 0.0), axis=-1, keepdims=True)
    rank2 = jnp.sum(jnp.where(oh2, prefix, 0.0), axis=-1, keepdims=True)
    carry = carry_ref[...] + jnp.sum(oh, axis=0, keepdims=True)
    carry_ref[...] = carry
    cnt_ref[...] = carry
    li = lax.broadcasted_iota(jnp.int32, lg.shape, 1)
    fields = (i1 - ROUTER_OFF, i2 - ROUTER_OFF, rank1, rank2, w1, w2)
    route = jnp.zeros(lg.shape, F32)
    for k, f in enumerate(fields):
        route = jnp.where(li == k, f, route)
    route_ref[rows, :] = route
    route_t_ref[:, rows] = route.T[0:ROUTE_ROWS, :]


def _merge(x, oa, ob, oc, gt, wb, wo, g2, wrh, wrl, br, tri, tm):
    T = x.shape[0]
    row = lambda i: (i, 0)
    full = lambda i: (0, 0)
    const = lambda shape: pl.BlockSpec(shape, full, pipeline_mode=pl.Buffered(1))
    return pl.pallas_call(
        functools.partial(_merge_kernel, tm=tm, rblk=min(MERGE_RBLK, tm)), grid=(T // tm,),
        in_specs=[pl.BlockSpec((tm, D_MODEL), row), pl.BlockSpec((tm, W_A), row),
                  pl.BlockSpec((tm, W_B), row), pl.BlockSpec((tm, W_C), row),
                  pl.BlockSpec((tm, N_BRANCH * D_MODEL), row),
                  const((D_MODEL, D_MODEL)), const((D_MODEL, D_MODEL)),
                  pl.BlockSpec((1, D_MODEL), full),
                  const((D_MODEL, LANES)), const((D_MODEL, LANES)),
                  pl.BlockSpec((1, LANES), full), pl.BlockSpec((tm, tm), full)],
        out_specs=(pl.BlockSpec((tm, D_MODEL), row), pl.BlockSpec((tm, D_MODEL // 2), row),
                   pl.BlockSpec((tm, LANES), row), pl.BlockSpec((ROUTE_ROWS, tm), lambda i: (0, i)),
                   pl.BlockSpec((1, LANES), full)),
        out_shape=(jax.ShapeDtypeStruct((T, D_MODEL), F32),
                   jax.ShapeDtypeStruct((T, D_MODEL // 2), jnp.uint32),
                   jax.ShapeDtypeStruct((T, LANES), F32), jax.ShapeDtypeStruct((ROUTE_ROWS, T), F32),
                   jax.ShapeDtypeStruct((1, LANES), F32)),
        scratch_shapes=[pltpu.VMEM((1, LANES), F32)],
        compiler_params=_cparams(("arbitrary",)), name="merge",
    )(x, oa, ob, oc, gt, wb, wo, g2, wrh, wrl, br, tri)


def _row_copy(src_ref, src_row, dst_ref, dst_row, sem):
    return pltpu.make_async_copy(src_ref.at[pl.ds(src_row, 1)], dst_ref.at[pl.ds(dst_row, 1)], sem)


def _slot_row(offs_ref, e_ref, rk_ref, j):
    return offs_ref[e_ref[0, 0, j]] + rk_ref[0, 0, j]


def _wait_rows(src_ref, dst_ref, sem, n):
    def wait(r, c):
        _row_copy(src_ref, 0, dst_ref, 0, sem).wait()
        return c

    lax.fori_loop(0, n, wait, 0, unroll=DMA_UNROLL)


def _dispatch_kernel(offs_ref, e_ref, rk_ref, hp_ref, xs_in_ref, xs_ref, sem, *, tm):
    del xs_in_ref

    def start(r, c):
        _row_copy(hp_ref, r, xs_ref, _slot_row(offs_ref, e_ref, rk_ref, r), sem).start()
        _row_copy(hp_ref, r, xs_ref, _slot_row(offs_ref, e_ref, rk_ref, tm + r), sem).start()
        return c

    lax.fori_loop(0, tm, start, 0, unroll=DMA_UNROLL)
    _wait_rows(hp_ref, xs_ref, sem, 2 * tm)


def _slot_specs(tm):
    smem = lambda: pl.BlockSpec((1, 1, 2 * tm), lambda i, offs: (i, 0, 0), memory_space=pltpu.SMEM)
    return [smem(), smem()]


def _dispatch(offs, eidx, ridx, hp, xs0, tm):
    T, C = hp.shape
    grid_spec = pltpu.PrefetchScalarGridSpec(
        num_scalar_prefetch=1, grid=(T // tm,),
        in_specs=_slot_specs(tm) + [pl.BlockSpec((tm, C), lambda i, offs: (i, 0)),
                                    pl.BlockSpec(memory_space=pl.ANY)],
        out_specs=pl.BlockSpec(memory_space=pl.ANY),
        scratch_shapes=[pltpu.SemaphoreType.DMA(())],
    )
    return pl.pallas_call(
        functools.partial(_dispatch_kernel, tm=tm), grid_spec=grid_spec,
        out_shape=jax.ShapeDtypeStruct(xs0.shape, xs0.dtype),
        input_output_aliases={4: 0},
        compiler_params=_cparams(("arbitrary",)), name="moe_dispatch",
    )(offs, eidx, ridx, hp, xs0)


def _expert_kernel(te_ref, nv_ref, xs_ref, wg_ref, wu_ref, wd_ref, ys_ref):
    del te_ref
    t = pl.program_id(0)

    @pl.when(t < nv_ref[0])
    def _():
        lo, hi = _unpack_halves(xs_ref[...])
        lo = lo.astype(BF16)
        hi = hi.astype(BF16)
        half = D_MODEL // 2
        a = (jnp.dot(lo, wg_ref[0:half, :], preferred_element_type=F32)
             + jnp.dot(hi, wg_ref[half:D_MODEL, :], preferred_element_type=F32))
        b = (jnp.dot(lo, wu_ref[0:half, :], preferred_element_type=F32)
             + jnp.dot(hi, wu_ref[half:D_MODEL, :], preferred_element_type=F32))
        hh = ((a * jax.nn.sigmoid(a)) * b).astype(BF16)
        ys_ref[...] = _pack_halves(jnp.dot(hh, wd_ref[...], preferred_element_type=F32))

    @pl.when(t >= nv_ref[0])
    def _():
        ys_ref[...] = jnp.zeros(ys_ref.shape, ys_ref.dtype)


def _experts(tile_expert, n_valid, xs, wg, wu, wd, layer):
    NP, C = xs.shape
    wspec = lambda shape: pl.BlockSpec((None, None) + shape, lambda t, te, nv: (layer, te[t], 0, 0))
    grid_spec = pltpu.PrefetchScalarGridSpec(
        num_scalar_prefetch=2, grid=(NP // MOE_TM,),
        in_specs=[pl.BlockSpec((MOE_TM, C), lambda t, te, nv: (t, 0)),
                  wspec((D_MODEL, D_EXPERT)), wspec((D_MODEL, D_EXPERT)), wspec((D_EXPERT, D_MODEL))],
        out_specs=pl.BlockSpec((MOE_TM, C), lambda t, te, nv: (t, 0)),
    )
    return pl.pallas_call(
        _expert_kernel, grid_spec=grid_spec, out_shape=jax.ShapeDtypeStruct(xs.shape, xs.dtype),
        compiler_params=_cparams(("arbitrary",)), name="moe_experts",
    )(tile_expert, n_valid, xs, wg, wu, wd)


def _combine_kernel(offs_ref, e_ref, rk_ref, h_ref, route_ref, ys_ref, y_ref, r1_ref, r2_ref, sem,
                    *, tm):
    def start(r, c):
        _row_copy(ys_ref, _slot_row(offs_ref, e_ref, rk_ref, r), r1_ref, r, sem).start()
        _row_copy(ys_ref, _slot_row(offs_ref, e_ref, rk_ref, tm + r), r2_ref, r, sem).start()
        return c

    lax.fori_loop(0, tm, start, 0, unroll=DMA_UNROLL)
    _wait_rows(ys_ref, r1_ref, sem, 2 * tm)
    route = route_ref[...]
    w1 = route[:, 4:5]
    w2 = route[:, 5:6]
    a_lo, a_hi = _unpack_halves(r1_ref[...])
    b_lo, b_hi = _unpack_halves(r2_ref[...])
    half = D_MODEL // 2
    y_ref[:, 0:half] = h_ref[:, 0:half] + w1 * a_lo + w2 * b_lo
    y_ref[:, half:D_MODEL] = h_ref[:, half:D_MODEL] + w1 * a_hi + w2 * b_hi


def _combine(offs, eidx, ridx, h, route, ys, tm):
    T = h.shape[0]
    C = ys.shape[1]
    row = lambda i, offs: (i, 0)
    grid_spec = pltpu.PrefetchScalarGridSpec(
        num_scalar_prefetch=1, grid=(T // tm,),
        in_specs=_slot_specs(tm) + [pl.BlockSpec((tm, D_MODEL), row), pl.BlockSpec((tm, LANES), row),
                                    pl.BlockSpec(memory_space=pl.ANY)],
        out_specs=pl.BlockSpec((tm, D_MODEL), row),
        scratch_shapes=[pltpu.VMEM((tm, C), jnp.uint32), pltpu.VMEM((tm, C), jnp.uint32),
                        pltpu.SemaphoreType.DMA(())],
    )
    return pl.pallas_call(
        functools.partial(_combine_kernel, tm=tm), grid_spec=grid_spec,
        out_shape=jax.ShapeDtypeStruct((T, D_MODEL), F32),
        compiler_params=_cparams(("arbitrary",)), name="moe_combine",
    )(offs, eidx, ridx, h, route, ys)


def _moe_sparse(h, hp, route, route_t, counts, wg, wu, wd, layer, tm):
    T = h.shape[0]
    n_pad = 2 * T + N_EXPERTS * MOE_TM
    cnt = counts[0, ROUTER_OFF:ROUTER_OFF + N_EXPERTS].astype(jnp.int32)
    padded = ((cnt + MOE_TM - 1) // MOE_TM) * MOE_TM
    ends = jnp.cumsum(padded)
    offs = ends - padded
    tile_start = jnp.arange(n_pad // MOE_TM, dtype=jnp.int32) * MOE_TM
    tile_expert = jnp.minimum(jnp.sum(tile_start[:, None] >= ends[None, :], axis=1),
                              N_EXPERTS - 1).astype(jnp.int32)
    n_valid = (ends[-1] // MOE_TM).astype(jnp.int32).reshape(1)
    per_step = lambda rows: jnp.transpose(rows.astype(jnp.int32).reshape(2, T // tm, tm),
                                          (1, 0, 2)).reshape(T // tm, 1, 2 * tm)
    eidx = per_step(route_t[0:2])
    ridx = per_step(route_t[2:4])
    offs = offs.astype(jnp.int32)
    xs0 = jnp.zeros((n_pad, D_MODEL // 2), jnp.uint32)
    xs = _dispatch(offs, eidx, ridx, hp, xs0, tm)
    ys = _experts(tile_expert, n_valid, xs, wg, wu, wd, layer)
    return _combine(offs, eidx, ridx, h, route, ys, tm)


def _pick(n, cands):
    for c in cands:
        if n % c == 0:
            return c
    raise ValueError(f"no tile for {n}")


def _layer(x, l, depth, P, w, kv_all, cache_k, cache_v, pool_prev, B, L, is_prompt):
    T = B * L
    lam_init = 0.8 - 0.6 * math.exp(-0.3 * l)
    tm = _pick(L if is_prompt else T, (1024, 512, 256, 128))
    q, kf, kb, vf, vb, bc, xn = _inproj(x, P["norm1"], w["w_in"], w["e_mat"], P["gq"], P["gk"],
                                        w["w_kt"], P["gk_rows"], kv_all, l, depth, tm,
                                        L if is_prompt else None)
    gt = _gates(xn, w["w_gates"], tm)

    if is_prompt:
        tq = _pick(L, (2048, 1024, 512, 256, 128, 64))
        oa = _attn_prompt(q, kb, vb, P["lam"], P["g_sub"], P["attn_fast"], B, L, tq, lam_init)
    else:
        oa = _attn_sample(q, kb, vb, cache_k, cache_v, l, P["lam"], P["g_sub"], B, L, lam_init)

    xc = bc[:, 2 * W_B:].reshape(B, L, W_C)
    if is_prompt:
        tmb = _pick(L, (512, 256, 128))
        csz = SG_CHUNK
        nt = L // tmb
        tails = xc.reshape(B, nt, tmb, W_C)[:, :, tmb - HALO:, :]
        halo = jnp.concatenate([jnp.zeros((B, 1, HALO, W_C), F32), tails[:, :-1]], axis=1)
        halo = halo.reshape(B * nt, HALO, W_C)
        pos0 = 0
        pool_new = xc[:, L - POOL_STATE:, :]
    else:
        tmb = L
        csz = L
        halo = jnp.concatenate([jnp.zeros((B, HALO - POOL_STATE, W_C), F32), pool_prev], axis=1)
        pos0 = cache_v.shape[2]
        pool_new = jnp.concatenate([pool_prev, xc], axis=1)[:, -POOL_STATE:, :]
    ob, oc, vbn = _branches(bc, halo, P["g_vb"], w["w_sp"][:, :csz, :csz], w["b_sp_full"][:csz],
                            w["w_pool"], P["pool_scale"], tmb, csz, L, pos0)

    tm2 = _pick(T, (MERGE_TM,))
    h, hp, route, route_t, counts = _merge(x, oa, ob, oc, gt, w["w_branch"], w["w_out"], P["norm2"],
                                           w["wr_hi"], w["wr_lo"], P["b_r"], w["tri_rank"], tm2)
    y = _moe_sparse(h, hp, route, route_t, counts, w["w_e_gate"], w["w_e_up"], w["w_e_down"], l, tm2)
    return y, (kf, vf), pool_new, vbn


def kernel(x_prompt, x_sample, cache_k, cache_v, state_pool, norm1, w_in, g_q, g_k, lam, g_sub,
           g_vb, w_sp, b_sp, w_pool, pool_scale, w_branch, w_out, norm2, w_rg, b_rg, w_re, b_re,
           w_e_gate, w_e_up, w_e_down):
    depth = w_in.shape[0]
    B, S, D = x_prompt.shape
    Bs, Ls, _ = x_sample.shape
    PL = cache_k.shape[2]
    ck = jnp.transpose(cache_k, (0, 1, 3, 4, 5, 2)).reshape(depth, Bs, W_A, PL)
    cv = cache_v.reshape(depth, Bs, PL, W_A)

    rep = IN_TN // HD_A
    gidx = jnp.arange(IN_TN) // HD_A
    e_mat = (gidx[:, None] == gidx[None, :]).astype(BF16)
    tri = jnp.tril(jnp.ones((SG_CHUNK, SG_CHUNK), bool))
    weg, weu, wed = w_e_gate.astype(BF16), w_e_up.astype(BF16), w_e_down.astype(BF16)
    ridx = jnp.arange(MERGE_TM)
    tri_rank = (ridx[None, :] < ridx[:, None]).astype(BF16)

    xp = x_prompt.reshape(B * S, D)
    xs = x_sample.reshape(Bs * Ls, D)
    outs = {k: [] for k in ("pp", "ps", "gs")}
    kv_p = (jnp.zeros((depth, B, W_A, S), F32), jnp.zeros((depth, B * S, W_A), F32))
    kv_s = (jnp.zeros((depth, Bs * Ls, W_A), F32), jnp.zeros((depth, Bs * Ls, W_A), F32))
    for l in range(depth):
        w_r = jnp.zeros((D, LANES), F32)
        w_r = w_r.at[:, :N_EXPERT_GROUPS].set(w_rg[l]).at[:, ROUTER_OFF:ROUTER_OFF + N_EXPERTS].set(w_re[l])
        wr_hi = w_r.astype(BF16)
        wr_lo = (w_r - wr_hi.astype(F32)).astype(BF16)
        b_r = jnp.zeros((1, LANES), F32)
        b_r = b_r.at[0, :N_EXPERT_GROUPS].set(b_rg[l]).at[0, ROUTER_OFF:ROUTER_OFF + N_EXPERTS].set(b_re[l])
        score_bound = HD_A * QK_SCALE * jnp.max(jnp.abs(g_q[l])) * jnp.max(jnp.abs(g_k[l]))
        attn_fast = (score_bound <= MAX_UNSHIFTED_SCORE).astype(jnp.int32).reshape(1)
        P = dict(norm1=norm1[l][None], norm2=norm2[l][None], attn_fast=attn_fast,
                 gq=(jnp.tile(g_q[l], rep) * QK_SCALE)[None], gk=jnp.tile(g_k[l], rep)[None],
                 gk_rows=jnp.broadcast_to(jnp.tile(g_k[l], rep)[:, None], (IN_TN, LANES)),
                 lam=lam[l], g_sub=g_sub[l][None], g_vb=g_vb[l][None],
                 pool_scale=pool_scale[l][None], b_r=b_r)
        w = dict(w_in=w_in[l, :, :GATE_COL0].astype(BF16), w_gates=w_in[l, :, GATE_COL0:].astype(BF16),
                 w_kt=jnp.transpose(w_in[l, :, W_A:2 * W_A]).astype(BF16), e_mat=e_mat,
                 w_sp=jnp.where(tri, w_sp[l], 0.0).astype(BF16),
                 b_sp_full=jnp.repeat(jnp.transpose(b_sp[l]), LANES, axis=1),
                 w_pool=w_pool[l].astype(BF16), w_branch=w_branch[l].astype(BF16),
                 w_out=w_out[l].astype(BF16), wr_hi=wr_hi, wr_lo=wr_lo,
                 w_e_gate=weg, w_e_up=weu, w_e_down=wed, tri_rank=tri_rank)
        xp, kv_p, pool1, _ = _layer(xp, l, depth, P, w, kv_p, None, None, None, B, S, True)
        xs, kv_s, pool2, vb2 = _layer(xs, l, depth, P, w, kv_s, ck, cv, state_pool[l], Bs, Ls, False)
        outs["pp"].append(pool1)
        outs["ps"].append(pool2)
        outs["gs"].append(vb2.reshape(Bs, Ls, W_B))
    st = lambda k: jnp.stack(outs[k])
    return (xp.reshape(B, S, D), xs.reshape(Bs, Ls, D),
            jnp.transpose(kv_p[0].reshape(depth, B, N_HEADS_A, 2, HD_A, S), (0, 1, 5, 2, 3, 4)),
            kv_p[1].reshape(depth, B, S, N_HEADS_A, 2 * HD_A), st("pp"),
            kv_s[0].reshape(depth, Bs, Ls, N_HEADS_A, 2, HD_A),
            kv_s[1].reshape(depth, Bs, Ls, N_HEADS_A, 2 * HD_A), st("ps"), st("gs"))
```

```python
import functools
import math

import jax
import jax.numpy as jnp
from jax import lax
from jax.experimental import pallas as pl
from jax.experimental.pallas import tpu as pltpu

F32 = jnp.float32
BF16 = jnp.bfloat16

D_MODEL = 2048
CHUNK = 64
N_HEADS_A = 8
HD_A = 64
W_A = N_HEADS_A * 2 * HD_A
SG_CHUNK = 128
N_GROUPS_B = 4
W_B = 512
POOL_WINDOWS = (2, 4, 8, 16)
N_GROUPS_C = 4
W_C = 512
POOL_STATE = 15
N_BRANCH = 3
IN_COLS = W_A * 3 + W_B * 2 + W_C + N_BRANCH * D_MODEL
N_EXPERT_GROUPS = 4
EXPERTS_PER_GROUP = 8
N_EXPERTS = N_EXPERT_GROUPS * EXPERTS_PER_GROUP
D_EXPERT = 256
EPS = 1e-6

LANES = 128
SUBLANES = 8
HALO = 16
NEG = -1e30
ATTN_RBLK = 512
QK_SCALE = (HD_A ** -0.5) * math.log2(math.e)
MAX_UNSHIFTED_SCORE = 48.0
VMEM_LIMIT = 58 * 1024 * 1024
ROUTER_OFF = N_EXPERT_GROUPS

IN_TN = 512
_JQ, _JK, _JV, _JBC, _JG = 0, 2, 4, 6, 9
GATE_COL0 = _JG * IN_TN
GATES_TN = 1024
GATES_RBLK = 256
INPROJ_RBLK = 512
MERGE_TM = 256
MERGE_RBLK = 256
MOE_TM = 256
ROUTE_ROWS = 8
DMA_UNROLL = 8


def _cparams(sem):
    return pltpu.CompilerParams(dimension_semantics=sem, vmem_limit_bytes=VMEM_LIMIT)


def _inproj_kernel(x_ref, g1_ref, w_ref, e_ref, gq_ref, gk_ref, wkt_ref, gkt_ref, k_all_in, v_all_in,
                   q_ref, kf_ref, kb_ref, vf_ref, vb_ref, bc_ref, xn_ref, *, tm, rblk, k_transposed):
    del k_all_in, v_all_in
    j = pl.program_id(1)

    @pl.when(j == 0)
    def _():
        x = x_ref[...]
        ms = jnp.mean(x * x, axis=-1, keepdims=True)
        xn_ref[...] = (x * lax.rsqrt(ms + EPS) * g1_ref[...]).astype(BF16)

    def headnorm(p, g):
        ss = jnp.dot((p * p).astype(BF16), e_ref[...], preferred_element_type=F32)
        return p * lax.rsqrt(ss * (1.0 / HD_A) + EPS) * g

    def row_blocks(epilogue):
        def run():
            for r0 in range(0, tm, rblk):
                rows = slice(r0, r0 + rblk)
                epilogue(rows, jnp.dot(xn_ref[rows, :], w_ref[...], preferred_element_type=F32))
        return run

    def q_out(rows, p):
        q_ref[rows, :] = headnorm(p, gq_ref[...]).astype(BF16)

    def k_out(rows, p):
        kn = headnorm(p, gk_ref[...])
        kf_ref[rows, :] = kn
        kb_ref[rows, :] = kn.astype(BF16)

    def v_out(rows, p):
        vf_ref[rows, :] = p
        vb_ref[rows, :] = p.astype(BF16)

    def bc_out(rows, p):
        bc_ref[rows, :] = p

    def k_out_transposed():
        heads = IN_TN // HD_A
        for r0 in range(0, tm, rblk):
            rows = slice(r0, r0 + rblk)
            pt = lax.dot_general(wkt_ref[...], xn_ref[rows, :], (((1,), (1,)), ((), ())),
                                 preferred_element_type=F32)
            p3 = pt.reshape(heads, HD_A, rblk)
            ss = jnp.sum(p3 * p3, axis=1, keepdims=True)
            kn = (p3 * lax.rsqrt(ss * (1.0 / HD_A) + EPS)).reshape(IN_TN, rblk)
            kn = kn * jnp.tile(gkt_ref[...], (1, rblk // LANES))
            kf_ref[:, rows] = kn
            kb_ref[:, rows] = kn.astype(BF16)

    pl.when(j < _JK)(row_blocks(q_out))
    pl.when((j >= _JK) & (j < _JV))(k_out_transposed if k_transposed else row_blocks(k_out))
    pl.when((j >= _JV) & (j < _JBC))(row_blocks(v_out))
    pl.when(j >= _JBC)(row_blocks(bc_out))


def _gates_kernel(xn_ref, w_ref, gt_ref, *, tm, rblk):
    for r0 in range(0, tm, rblk):
        p = jnp.dot(xn_ref[r0:r0 + rblk, :], w_ref[...], preferred_element_type=F32)
        gt_ref[r0:r0 + rblk, :] = (0.5 * jnp.tanh(0.5 * p) + 0.5).astype(BF16)


def _gates(xn, w_g, tm):
    T = xn.shape[0]
    ncol = w_g.shape[1]
    return pl.pallas_call(
        functools.partial(_gates_kernel, tm=tm, rblk=min(GATES_RBLK, tm)),
        grid=(T // tm, ncol // GATES_TN),
        in_specs=[pl.BlockSpec((tm, D_MODEL), lambda i, j: (i, 0)),
                  pl.BlockSpec((D_MODEL, GATES_TN), lambda i, j: (0, j))],
        out_specs=pl.BlockSpec((tm, GATES_TN), lambda i, j: (i, j)),
        out_shape=jax.ShapeDtypeStruct((T, ncol), BF16),
        compiler_params=_cparams(("arbitrary", "arbitrary")), name="gates",
    )(xn, w_g)


def _inproj(x, g1, w_bf, e_mat, gq, gk, wkt, gkt, kv_all, layer, depth, tm, seq_k_transposed):
    T = x.shape[0]
    grid = (T // tm, _JG)

    def seg(j0, n):
        return lambda i, j: (i, jnp.clip(j - j0, 0, n - 1))

    def seg_l(j0, n):
        return lambda i, j: (layer, i, jnp.clip(j - j0, 0, n - 1))

    blk = lambda f: pl.BlockSpec((tm, IN_TN), f)
    blk_l = lambda f: pl.BlockSpec((None, tm, IN_TN), f)
    if seq_k_transposed is None:
        kf_shape, kb_shape = (depth, T, W_A), (T, W_A)
        kf_spec, kb_spec = blk_l(seg_l(_JK, 2)), blk(seg(_JK, 2))
    else:
        S = seq_k_transposed
        nsb = S // tm
        kf_shape, kb_shape = (depth, T // S, W_A, S), (T // S, W_A, S)
        kf_spec = pl.BlockSpec((None, None, IN_TN, tm),
                               lambda i, j: (layer, i // nsb, jnp.clip(j - _JK, 0, 1), i % nsb))
        kb_spec = pl.BlockSpec((None, IN_TN, tm),
                               lambda i, j: (i // nsb, jnp.clip(j - _JK, 0, 1), i % nsb))
    out_shape = (
        jax.ShapeDtypeStruct((T, W_A), BF16),
        jax.ShapeDtypeStruct(kf_shape, F32),
        jax.ShapeDtypeStruct(kb_shape, BF16),
        jax.ShapeDtypeStruct((depth, T, W_A), F32),
        jax.ShapeDtypeStruct((T, W_A), BF16),
        jax.ShapeDtypeStruct((T, 2 * W_B + W_C), F32),
        jax.ShapeDtypeStruct((T, D_MODEL), BF16),
    )
    out_specs = (blk(seg(_JQ, 2)), kf_spec, kb_spec, blk_l(seg_l(_JV, 2)),
                 blk(seg(_JV, 2)), blk(seg(_JBC, 3)),
                 pl.BlockSpec((tm, D_MODEL), lambda i, j: (i, 0)))
    in_specs = [
        pl.BlockSpec((tm, D_MODEL), lambda i, j: (i, 0)),
        pl.BlockSpec((1, D_MODEL), lambda i, j: (0, 0)),
        pl.BlockSpec((D_MODEL, IN_TN), lambda i, j: (0, j)),
        pl.BlockSpec((IN_TN, IN_TN), lambda i, j: (0, 0)),
        pl.BlockSpec((1, IN_TN), lambda i, j: (0, 0)),
        pl.BlockSpec((1, IN_TN), lambda i, j: (0, 0)),
        pl.BlockSpec((IN_TN, D_MODEL), lambda i, j: (jnp.clip(j - _JK, 0, 1), 0)),
        pl.BlockSpec((IN_TN, LANES), lambda i, j: (0, 0)),
    ]
    in_specs += [pl.BlockSpec(memory_space=pl.ANY)] * 2
    return pl.pallas_call(
        functools.partial(_inproj_kernel, tm=tm, rblk=min(INPROJ_RBLK, tm),
                          k_transposed=seq_k_transposed is not None),
        grid=grid, in_specs=in_specs, out_specs=out_specs, out_shape=out_shape,
        input_output_aliases={8: 1, 9: 3},
        compiler_params=_cparams(("arbitrary", "arbitrary")), name="inproj",
    )(x, g1, w_bf, e_mat, gq, gk, wkt, gkt, *kv_all)


def _lambda_value(lam_ref, lam_init):
    lp = lam_ref[...]
    a = jnp.sum(lp[0:1] * lp[1:2], axis=-1, keepdims=True)
    b = jnp.sum(lp[2:3] * lp[3:4], axis=-1, keepdims=True)
    return jnp.exp(a) - jnp.exp(b) + lam_init


def _stack_q(q, qs_ref, tq):
    lane = lax.broadcasted_iota(jnp.int32, q.shape, 1)
    zero = jnp.zeros_like(q)
    qs_ref[0:tq, :] = jnp.where(lane < HD_A, q, zero)
    qs_ref[tq:2 * tq, :] = jnp.where(lane >= HD_A, q, zero)


def _attn_finish(acc, l, lam, gsub, tq, lam_init):
    o = acc / l
    o = o[0:tq] - lam * o[tq:2 * tq]
    ms = jnp.mean(o * o, axis=-1, keepdims=True)
    return o * lax.rsqrt(ms + EPS) * gsub * (1.0 - lam_init)


def _attn_prompt_kernel(qi_ref, kj_ref, fast_ref, q_ref, k_ref, v_ref, lam_ref, gsub_ref, o_ref,
                        qs_ref, m_ref, l_ref, acc_ref, *, tq, rblk, lam_init):
    t = pl.program_id(2)
    qi = qi_ref[t]
    kj = kj_ref[t]
    fast = fast_ref[0] != 0

    @pl.when(kj == 0)
    def _():
        _stack_q(q_ref[...], qs_ref, tq)
        m_ref[...] = jnp.full(m_ref.shape, NEG, F32)
        l_ref[...] = jnp.zeros(l_ref.shape, F32)
        acc_ref[...] = jnp.zeros(acc_ref.shape, F32)

    def step(masked, bounded):
        for rb in range(2 * tq // rblk):
            r0 = rb * rblk
            q0 = r0 % tq
            nk = min(tq, q0 + rblk) if masked else tq
            s = jnp.dot(qs_ref[r0:r0 + rblk, :], k_ref[:, 0:nk],
                        preferred_element_type=F32)
            if masked:
                r = lax.broadcasted_iota(jnp.int32, s.shape, 0) + q0
                c = lax.broadcasted_iota(jnp.int32, s.shape, 1)
                s = jnp.where((c // CHUNK) <= (r // CHUNK), s, NEG)
            if bounded:
                p = jnp.exp2(s)
                l_ref[r0:r0 + rblk, :] += jnp.sum(p, axis=-1, keepdims=True)
                acc_ref[r0:r0 + rblk, :] += jnp.dot(p.astype(BF16), v_ref[0:nk, :],
                                                    preferred_element_type=F32)
                continue
            m_prev = m_ref[r0:r0 + rblk, :]
            m_new = jnp.maximum(m_prev, jnp.max(s, axis=-1, keepdims=True))
            alpha = jnp.exp2(m_prev - m_new)
            p = jnp.exp2(s - jnp.tile(m_new, (1, nk // LANES)))
            l_ref[r0:r0 + rblk, :] = alpha * l_ref[r0:r0 + rblk, :] + jnp.sum(p, axis=-1, keepdims=True)
            acc_ref[r0:r0 + rblk, :] = alpha * acc_ref[r0:r0 + rblk, :] + jnp.dot(
                p.astype(BF16), v_ref[0:nk, :], preferred_element_type=F32)
            m_ref[r0:r0 + rblk, :] = m_new

    for bounded in (True, False):
        use = fast if bounded else jnp.logical_not(fast)
        pl.when((kj < qi) & use)(functools.partial(step, False, bounded))
        pl.when((kj == qi) & use)(functools.partial(step, True, bounded))

    @pl.when(kj == qi)
    def _():
        lam = _lambda_value(lam_ref, lam_init)
        o_ref[...] = _attn_finish(acc_ref[...], l_ref[...], lam, gsub_ref[...], tq,
                                  lam_init).astype(o_ref.dtype)


def _attn_prompt(q, k, v, lam_p, gsub, fast, B, S, tq, lam_init):
    nq = S // tq
    pairs = [(i, j) for i in range(nq) for j in range(i + 1)]
    qi = jnp.asarray([p[0] for p in pairs], jnp.int32)
    kj = jnp.asarray([p[1] for p in pairs], jnp.int32)
    grid_spec = pltpu.PrefetchScalarGridSpec(
        num_scalar_prefetch=3,
        grid=(B, N_HEADS_A, len(pairs)),
        in_specs=[
            pl.BlockSpec((tq, LANES), lambda b, h, t, qi, kj, f: (b * nq + qi[t], h)),
            pl.BlockSpec((None, LANES, tq), lambda b, h, t, qi, kj, f: (b, h, kj[t])),
            pl.BlockSpec((tq, LANES), lambda b, h, t, qi, kj, f: (b * nq + kj[t], h)),
            pl.BlockSpec((4, HD_A), lambda b, h, t, qi, kj, f: (0, 0)),
            pl.BlockSpec((1, LANES), lambda b, h, t, qi, kj, f: (0, 0)),
        ],
        out_specs=pl.BlockSpec((tq, LANES), lambda b, h, t, qi, kj, f: (b * nq + qi[t], h)),
        scratch_shapes=[pltpu.VMEM((2 * tq, LANES), BF16), pltpu.VMEM((2 * tq, LANES), F32),
                        pltpu.VMEM((2 * tq, LANES), F32), pltpu.VMEM((2 * tq, LANES), F32)],
    )
    return pl.pallas_call(
        functools.partial(_attn_prompt_kernel, tq=tq, rblk=min(ATTN_RBLK, tq), lam_init=lam_init),
        grid_spec=grid_spec, out_shape=jax.ShapeDtypeStruct((B * S, W_A), BF16),
        compiler_params=_cparams(("arbitrary", "arbitrary", "arbitrary")), name="attn_prompt",
    )(qi, kj, fast, q, k, v, lam_p, gsub)


def _attn_sample_kernel(q_ref, kc_ref, vc_ref, kn_ref, vn_ref, lam_ref, gsub_ref, o_ref,
                        qs_ref, *, tq, lam_init):
    _stack_q(q_ref[...], qs_ref, tq)
    qs = qs_ref[...]
    dn = (((1,), (1,)), ((), ()))
    s_c = jnp.dot(qs, kc_ref[...].astype(BF16), preferred_element_type=F32)
    s_n = lax.dot_general(qs, kn_ref[...], dn, preferred_element_type=F32)
    m = jnp.maximum(jnp.max(s_c, axis=-1, keepdims=True), jnp.max(s_n, axis=-1, keepdims=True))
    p_c = jnp.exp2(s_c - m)
    p_n = jnp.exp2(s_n - m)
    l = jnp.sum(p_c, axis=-1, keepdims=True) + jnp.sum(p_n, axis=-1, keepdims=True)
    acc = (jnp.dot(p_c.astype(BF16), vc_ref[...].astype(BF16), preferred_element_type=F32)
           + jnp.dot(p_n.astype(BF16), vn_ref[...], preferred_element_type=F32))
    lam = _lambda_value(lam_ref, lam_init)
    o_ref[...] = _attn_finish(acc, l, lam, gsub_ref[...], tq, lam_init).astype(o_ref.dtype)


def _attn_sample(q, kn, vn, cache_k, cache_v, layer, lam_p, gsub, B, L, lam_init):
    P = cache_v.shape[2]
    cspec = pl.BlockSpec((None, None, P, LANES), lambda b, h: (layer, b, 0, h))
    kspec = pl.BlockSpec((None, None, LANES, P), lambda b, h: (layer, b, h, 0))
    nspec = pl.BlockSpec((L, LANES), lambda b, h: (b, h))
    return pl.pallas_call(
        functools.partial(_attn_sample_kernel, tq=L, lam_init=lam_init),
        grid=(B, N_HEADS_A),
        in_specs=[nspec, kspec, cspec, nspec, nspec,
                  pl.BlockSpec((4, HD_A), lambda b, h: (0, 0)),
                  pl.BlockSpec((1, LANES), lambda b, h: (0, 0))],
        out_specs=nspec,
        out_shape=jax.ShapeDtypeStruct((B * L, W_A), BF16),
        scratch_shapes=[pltpu.VMEM((2 * L, LANES), BF16)],
        compiler_params=_cparams(("arbitrary", "arbitrary")), name="attn_sample",
    )(q, cache_k, cache_v, kn, vn, lam_p, gsub)


def _branches_kernel(bc_ref, halo_ref, gvb_ref, wsp_ref, bsp_ref, wpool_ref, pscale_ref,
                     ob_ref, oc_ref, vbn_ref, *, tm, csz, seq, pos0):
    i = pl.program_id(0)
    u = bc_ref[:, 0:W_B]
    vb = bc_ref[:, W_B:2 * W_B]
    xc = bc_ref[:, 2 * W_B:2 * W_B + W_C]

    ms = jnp.mean(vb * vb, axis=-1, keepdims=True)
    vbn = vb * lax.rsqrt(ms + EPS) * gvb_ref[...]
    vbn_ref[...] = vbn
    vbn_bf = vbn.astype(BF16)
    for c in range(tm // csz):
        r0 = c * csz
        for g in range(N_GROUPS_B):
            c0 = g * LANES
            z = jnp.dot(wsp_ref[g], vbn_bf[r0:r0 + csz, c0:c0 + LANES],
                        preferred_element_type=F32) + bsp_ref[:, c0:c0 + LANES]
            ob_ref[r0:r0 + csz, c0:c0 + LANES] = (u[r0:r0 + csz, c0:c0 + LANES] * z).astype(BF16)

    ext = jnp.concatenate([halo_ref[0], xc], axis=0)
    row = lax.broadcasted_iota(jnp.int32, (tm, 1), 0)
    pos = (pos0 + (i * tm) % seq + row).astype(F32)
    acc = ext
    shift = 1
    for g, w in enumerate(POOL_WINDOWS):
        while shift < w:
            acc = acc + pltpu.roll(acc, shift, axis=0)
            shift *= 2
        c0 = g * LANES
        cnt = jnp.minimum(float(w), pos + 1.0)
        pooled = acc[HALO:HALO + tm, c0:c0 + LANES] / cnt
        diff = (pooled - xc[:, c0:c0 + LANES]).astype(BF16)
        h = jnp.dot(diff, wpool_ref[g], preferred_element_type=F32)
        oc_ref[:, c0:c0 + LANES] = (h * pscale_ref[:, c0:c0 + LANES]).astype(BF16)


def _branches(bc, halo, gvb, wsp, bsp_full, wpool, pscale, tm, csz, seq, pos0):
    T = bc.shape[0]
    row = lambda i: (i, 0)
    full2 = lambda i: (0, 0)
    full3 = lambda i: (0, 0, 0)
    return pl.pallas_call(
        functools.partial(_branches_kernel, tm=tm, csz=csz, seq=seq, pos0=pos0),
        grid=(T // tm,),
        in_specs=[pl.BlockSpec((tm, 2 * W_B + W_C), row),
                  pl.BlockSpec((1, HALO, W_C), lambda i: (i, 0, 0)),
                  pl.BlockSpec((1, W_B), full2),
                  pl.BlockSpec((N_GROUPS_B, csz, csz), full3),
                  pl.BlockSpec((csz, W_B), full2),
                  pl.BlockSpec((N_GROUPS_C, LANES, LANES), full3),
                  pl.BlockSpec((1, W_C), full2)],
        out_specs=(pl.BlockSpec((tm, W_B), row), pl.BlockSpec((tm, W_C), row),
                   pl.BlockSpec((tm, W_B), row)),
        out_shape=(jax.ShapeDtypeStruct((T, W_B), BF16), jax.ShapeDtypeStruct((T, W_C), BF16),
                   jax.ShapeDtypeStruct((T, W_B), F32)),
        compiler_params=_cparams(("arbitrary",)), name="branches",
    )(bc, halo, gvb, wsp, bsp_full, wpool, pscale)


def _pack_halves(x):
    c = x.shape[1] // 2
    lo = lax.bitcast_convert_type(x[:, :c].astype(BF16).astype(F32), jnp.uint32)
    hi = lax.bitcast_convert_type(x[:, c:].astype(BF16).astype(F32), jnp.uint32)
    return (lo >> 16) | (hi & jnp.uint32(0xFFFF0000))


def _unpack_halves(u):
    lo = lax.bitcast_convert_type(u << 16, F32)
    hi = lax.bitcast_convert_type(u & jnp.uint32(0xFFFF0000), F32)
    return lo, hi


def _merge_kernel(x_ref, oa_ref, ob_ref, oc_ref, gt_ref, wb_ref, wo_ref, g2_ref,
                  wrh_ref, wrl_ref, br_ref, tri_ref, h_ref, hp_ref, route_ref, route_t_ref, cnt_ref,
                  carry_ref, *, tm, rblk):
    @pl.when(pl.program_id(0) == 0)
    def _():
        carry_ref[...] = jnp.zeros(carry_ref.shape, F32)

    for r0 in range(0, tm, rblk):
        _merge_rows(slice(r0, r0 + rblk), rblk, x_ref, oa_ref, ob_ref, oc_ref, gt_ref, wb_ref, wo_ref,
                    g2_ref, wrh_ref, wrl_ref, br_ref, tri_ref, h_ref, hp_ref, route_ref, route_t_ref,
                    cnt_ref, carry_ref)


def _merge_rows(rows, rblk, x_ref, oa_ref, ob_ref, oc_ref, gt_ref, wb_ref, wo_ref, g2_ref,
                wrh_ref, wrl_ref, br_ref, tri_ref, h_ref, hp_ref, route_ref, route_t_ref, cnt_ref,
                carry_ref):
    ta = jnp.dot(oa_ref[rows, :], wb_ref[0:W_A, :], preferred_element_type=F32)
    tb = jnp.dot(ob_ref[rows, :], wb_ref[W_A:W_A + W_B, :], preferred_element_type=F32)
    tc = jnp.dot(oc_ref[rows, :], wb_ref[W_A + W_B:W_A + W_B + W_C, :], preferred_element_type=F32)
    merged = (gt_ref[rows, 0:D_MODEL].astype(F32) * ta
              + gt_ref[rows, D_MODEL:2 * D_MODEL].astype(F32) * tb
              + gt_ref[rows, 2 * D_MODEL:3 * D_MODEL].astype(F32) * tc)
    h = x_ref[rows, :] + jnp.dot(merged.astype(BF16), wo_ref[...], preferred_element_type=F32)
    h_ref[rows, :] = h
    ms = jnp.mean(h * h, axis=-1, keepdims=True)
    hn = h * lax.rsqrt(ms + EPS) * g2_ref[...]
    hp_ref[rows, :] = _pack_halves(hn)

    hi = hn.astype(BF16)
    lo = (hn - hi.astype(F32)).astype(BF16)
    lg = (jnp.dot(hi, wrh_ref[...], preferred_element_type=F32)
          + jnp.dot(lo, wrh_ref[...], preferred_element_type=F32)
          + jnp.dot(hi, wrl_ref[...], preferred_element_type=F32)) + br_ref[...]
    lane = lax.broadcasted_iota(jnp.int32, lg.shape, 1).astype(F32)
    big = float(LANES)
    gmask = lane < N_EXPERT_GROUPS
    lgg = jnp.where(gmask, lg, NEG)
    mg = jnp.max(lgg, axis=-1, keepdims=True)
    sg = jnp.sum(jnp.where(gmask, jnp.exp(lgg - mg), 0.0), axis=-1, keepdims=True)
    g_p = 1.0 / sg
    g_sel = jnp.min(jnp.where(lgg == mg, lane, big), axis=-1, keepdims=True)
    e0 = ROUTER_OFF + g_sel * EXPERTS_PER_GROUP
    emask = (lane >= e0) & (lane < e0 + EXPERTS_PER_GROUP)
    le = jnp.where(emask, lg, NEG)
    me = jnp.max(le, axis=-1, keepdims=True)
    ee = jnp.where(emask, jnp.exp(le - me), 0.0)
    pe = ee / jnp.sum(ee, axis=-1, keepdims=True)
    pe = jnp.where(emask, pe, -1.0)
    top1 = jnp.max(pe, axis=-1, keepdims=True)
    i1 = jnp.min(jnp.where(pe == top1, lane, big), axis=-1, keepdims=True)
    pe2 = jnp.where(lane == i1, -1.0, pe)
    top2 = jnp.max(pe2, axis=-1, keepdims=True)
    i2 = jnp.min(jnp.where(pe2 == top2, lane, big), axis=-1, keepdims=True)
    den = top1 + top2
    w1 = g_p * (top1 / den)
    w2 = g_p * (top2 / den)

    oh1 = lane == i1
    oh2 = lane == i2
    oh = jnp.where(oh1 | oh2, 1.0, 0.0)
    prefix = carry_ref[...] + jnp.dot(tri_ref[0:rblk, 0:rblk], oh.astype(BF16),
                                      preferred_element_type=F32)
    rank1 = jnp.sum(jnp.where(oh1, prefix, 0.0), axis=-1, keepdims=True)
    rank2 = jnp.sum(jnp.where(oh2, prefix, 0.0), axis=-1, keepdims=True)
    carry = carry_ref[...] + jnp.sum(oh, axis=0, keepdims=True)
    carry_ref[...] = carry
    cnt_ref[...] = carry
    li = lax.broadcasted_iota(jnp.int32, lg.shape, 1)
    fields = (i1 - ROUTER_OFF, i2 - ROUTER_OFF, rank1, rank2, w1, w2)
    route = jnp.zeros(lg.shape, F32)
    for k, f in enumerate(fields):
        route = jnp.where(li == k, f, route)
    route_ref[rows, :] = route
    route_t_ref[:, rows] = route.T[0:ROUTE_ROWS, :]


def _merge(x, oa, ob, oc, gt, wb, wo, g2, wrh, wrl, br, tri, tm):
    T = x.shape[0]
    row = lambda i: (i, 0)
    full = lambda i: (0, 0)
    const = lambda shape: pl.BlockSpec(shape, full, pipeline_mode=pl.Buffered(1))
    return pl.pallas_call(
        functools.partial(_merge_kernel, tm=tm, rblk=min(MERGE_RBLK, tm)), grid=(T // tm,),
        in_specs=[pl.BlockSpec((tm, D_MODEL), row), pl.BlockSpec((tm, W_A), row),
                  pl.BlockSpec((tm, W_B), row), pl.BlockSpec((tm, W_C), row),
                  pl.BlockSpec((tm, N_BRANCH * D_MODEL), row),
                  const((D_MODEL, D_MODEL)), const((D_MODEL, D_MODEL)),
                  pl.BlockSpec((1, D_MODEL), full),
                  const((D_MODEL, LANES)), const((D_MODEL, LANES)),
                  pl.BlockSpec((1, LANES), full), pl.BlockSpec((tm, tm), full)],
        out_specs=(pl.BlockSpec((tm, D_MODEL), row), pl.BlockSpec((tm, D_MODEL // 2), row),
                   pl.BlockSpec((tm, LANES), row), pl.BlockSpec((ROUTE_ROWS, tm), lambda i: (0, i)),
                   pl.BlockSpec((1, LANES), full)),
        out_shape=(jax.ShapeDtypeStruct((T, D_MODEL), F32),
                   jax.ShapeDtypeStruct((T, D_MODEL // 2), jnp.uint32),
                   jax.ShapeDtypeStruct((T, LANES), F32), jax.ShapeDtypeStruct((ROUTE_ROWS, T), F32),
                   jax.ShapeDtypeStruct((1, LANES), F32)),
        scratch_shapes=[pltpu.VMEM((1, LANES), F32)],
        compiler_params=_cparams(("arbitrary",)), name="merge",
    )(x, oa, ob, oc, gt, wb, wo, g2, wrh, wrl, br, tri)


def _row_copy(src_ref, src_row, dst_ref, dst_row, sem):
    return pltpu.make_async_copy(src_ref.at[pl.ds(src_row, 1)], dst_ref.at[pl.ds(dst_row, 1)], sem)


def _for_row_groups(tm, body):
    def group(g, c):
        base = pl.multiple_of(g * SUBLANES, SUBLANES)
        for k in range(SUBLANES):
            body(base, k)
        return c

    lax.fori_loop(0, tm // SUBLANES, group, 0)


def _wait_rows(src_ref, dst_ref, sem, n):
    def wait(r, c):
        _row_copy(src_ref, 0, dst_ref, 0, sem).wait()
        return c

    lax.fori_loop(0, n, wait, 0, unroll=DMA_UNROLL)


def _dispatch_kernel(idx_ref, hp_ref, xs_in_ref, xs_ref, sem, *, tm):
    del xs_in_ref

    def start(base, k):
        src = hp_ref.at[pl.ds(base, SUBLANES)]
        _row_copy(src, k, xs_ref, idx_ref[0, 0, base + k], sem).start()
        _row_copy(src, k, xs_ref, idx_ref[0, 0, tm + base + k], sem).start()

    _for_row_groups(tm, start)
    _wait_rows(hp_ref, xs_ref, sem, 2 * tm)


def _slot_spec(tm):
    return pl.BlockSpec((1, 1, 2 * tm), lambda i: (i, 0, 0), memory_space=pltpu.SMEM)


def _dispatch(idx, hp, xs0, tm):
    T, C = hp.shape
    return pl.pallas_call(
        functools.partial(_dispatch_kernel, tm=tm), grid=(T // tm,),
        in_specs=[_slot_spec(tm), pl.BlockSpec((tm, C), lambda i: (i, 0)),
                  pl.BlockSpec(memory_space=pl.ANY)],
        out_specs=pl.BlockSpec(memory_space=pl.ANY),
        out_shape=jax.ShapeDtypeStruct(xs0.shape, xs0.dtype),
        scratch_shapes=[pltpu.SemaphoreType.DMA(())],
        input_output_aliases={2: 0},
        compiler_params=_cparams(("arbitrary",)), name="moe_dispatch",
    )(idx, hp, xs0)


def _expert_kernel(te_ref, nv_ref, xs_ref, wg_ref, wu_ref, wd_ref, ys_ref):
    del te_ref
    t = pl.program_id(0)

    @pl.when(t < nv_ref[0])
    def _():
        lo, hi = _unpack_halves(xs_ref[...])
        lo = lo.astype(BF16)
        hi = hi.astype(BF16)
        half = D_MODEL // 2
        a = (jnp.dot(lo, wg_ref[0:half, :], preferred_element_type=F32)
             + jnp.dot(hi, wg_ref[half:D_MODEL, :], preferred_element_type=F32))
        b = (jnp.dot(lo, wu_ref[0:half, :], preferred_element_type=F32)
             + jnp.dot(hi, wu_ref[half:D_MODEL, :], preferred_element_type=F32))
        hh = ((a * jax.nn.sigmoid(a)) * b).astype(BF16)
        ys_ref[...] = _pack_halves(jnp.dot(hh, wd_ref[...], preferred_element_type=F32))

    @pl.when(t >= nv_ref[0])
    def _():
        ys_ref[...] = jnp.zeros(ys_ref.shape, ys_ref.dtype)


def _experts(tile_expert, n_valid, xs, wg, wu, wd, layer):
    NP, C = xs.shape
    wspec = lambda shape: pl.BlockSpec((None, None) + shape, lambda t, te, nv: (layer, te[t], 0, 0))
    grid_spec = pltpu.PrefetchScalarGridSpec(
        num_scalar_prefetch=2, grid=(NP // MOE_TM,),
        in_specs=[pl.BlockSpec((MOE_TM, C), lambda t, te, nv: (t, 0)),
                  wspec((D_MODEL, D_EXPERT)), wspec((D_MODEL, D_EXPERT)), wspec((D_EXPERT, D_MODEL))],
        out_specs=pl.BlockSpec((MOE_TM, C), lambda t, te, nv: (t, 0)),
    )
    return pl.pallas_call(
        _expert_kernel, grid_spec=grid_spec, out_shape=jax.ShapeDtypeStruct(xs.shape, xs.dtype),
        compiler_params=_cparams(("arbitrary",)), name="moe_experts",
    )(tile_expert, n_valid, xs, wg, wu, wd)


def _combine_kernel(idx_ref, h_ref, route_ref, ys_ref, y_ref, r1_ref, r2_ref, sem, *, tm):
    def start(base, k):
        _row_copy(ys_ref, idx_ref[0, 0, base + k], r1_ref.at[pl.ds(base, SUBLANES)], k, sem).start()
        _row_copy(ys_ref, idx_ref[0, 0, tm + base + k], r2_ref.at[pl.ds(base, SUBLANES)], k,
                  sem).start()

    _for_row_groups(tm, start)
    _wait_rows(ys_ref, r1_ref, sem, 2 * tm)
    route = route_ref[...]
    w1 = route[:, 4:5]
    w2 = route[:, 5:6]
    a_lo, a_hi = _unpack_halves(r1_ref[...])
    b_lo, b_hi = _unpack_halves(r2_ref[...])
    half = D_MODEL // 2
    y_ref[:, 0:half] = h_ref[:, 0:half] + w1 * a_lo + w2 * b_lo
    y_ref[:, half:D_MODEL] = h_ref[:, half:D_MODEL] + w1 * a_hi + w2 * b_hi


def _combine(idx, h, route, ys, tm):
    T = h.shape[0]
    C = ys.shape[1]
    row = lambda i: (i, 0)
    return pl.pallas_call(
        functools.partial(_combine_kernel, tm=tm), grid=(T // tm,),
        in_specs=[_slot_spec(tm), pl.BlockSpec((tm, D_MODEL), row), pl.BlockSpec((tm, LANES), row),
                  pl.BlockSpec(memory_space=pl.ANY)],
        out_specs=pl.BlockSpec((tm, D_MODEL), row),
        out_shape=jax.ShapeDtypeStruct((T, D_MODEL), F32),
        scratch_shapes=[pltpu.VMEM((tm, C), jnp.uint32), pltpu.VMEM((tm, C), jnp.uint32),
                        pltpu.SemaphoreType.DMA(())],
        compiler_params=_cparams(("arbitrary",)), name="moe_combine",
    )(idx, h, route, ys)


def _moe_sparse(h, hp, route, route_t, counts, wg, wu, wd, layer, tm):
    T = h.shape[0]
    n_pad = 2 * T + N_EXPERTS * MOE_TM
    cnt = counts[0, ROUTER_OFF:ROUTER_OFF + N_EXPERTS].astype(jnp.int32)
    padded = ((cnt + MOE_TM - 1) // MOE_TM) * MOE_TM
    ends = jnp.cumsum(padded)
    offs = ends - padded
    tile_start = jnp.arange(n_pad // MOE_TM, dtype=jnp.int32) * MOE_TM
    tile_expert = jnp.minimum(jnp.sum(tile_start[:, None] >= ends[None, :], axis=1),
                              N_EXPERTS - 1).astype(jnp.int32)
    n_valid = (ends[-1] // MOE_TM).astype(jnp.int32).reshape(1)
    e_idx = route_t[0:2].astype(jnp.int32)
    rows = route_t[2:4].astype(jnp.int32)
    for e in range(N_EXPERTS):
        rows = rows + jnp.where(e_idx == e, offs[e], 0)
    idx = jnp.transpose(rows.reshape(2, T // tm, tm), (1, 0, 2)).reshape(T // tm, 1, 2 * tm)
    xs0 = jnp.zeros((n_pad, D_MODEL // 2), jnp.uint32)
    xs = _dispatch(idx, hp, xs0, tm)
    ys = _experts(tile_expert, n_valid, xs, wg, wu, wd, layer)
    return _combine(idx, h, route, ys, tm)


def _pick(n, cands):
    for c in cands:
        if n % c == 0:
            return c
    raise ValueError(f"no tile for {n}")


def _layer(x, l, depth, P, w, kv_all, cache_k, cache_v, pool_prev, B, L, is_prompt):
    T = B * L
    lam_init = 0.8 - 0.6 * math.exp(-0.3 * l)
    tm = _pick(L if is_prompt else T, (1024, 512, 256, 128))
    q, kf, kb, vf, vb, bc, xn = _inproj(x, P["norm1"], w["w_in"], w["e_mat"], P["gq"], P["gk"],
                                        w["w_kt"], P["gk_rows"], kv_all, l, depth, tm,
                                        L if is_prompt else None)
    gt = _gates(xn, w["w_gates"], tm)

    if is_prompt:
        tq = _pick(L, (2048, 1024, 512, 256, 128, 64))
        oa = _attn_prompt(q, kb, vb, P["lam"], P["g_sub"], P["attn_fast"], B, L, tq, lam_init)
    else:
        oa = _attn_sample(q, kb, vb, cache_k, cache_v, l, P["lam"], P["g_sub"], B, L, lam_init)

    xc = bc[:, 2 * W_B:].reshape(B, L, W_C)
    if is_prompt:
        tmb = _pick(L, (512, 256, 128))
        csz = SG_CHUNK
        nt = L // tmb
        tails = xc.reshape(B, nt, tmb, W_C)[:, :, tmb - HALO:, :]
        halo = jnp.concatenate([jnp.zeros((B, 1, HALO, W_C), F32), tails[:, :-1]], axis=1)
        halo = halo.reshape(B * nt, HALO, W_C)
        pos0 = 0
        pool_new = xc[:, L - POOL_STATE:, :]
    else:
        tmb = L
        csz = L
        halo = jnp.concatenate([jnp.zeros((B, HALO - POOL_STATE, W_C), F32), pool_prev], axis=1)
        pos0 = cache_v.shape[2]
        pool_new = jnp.concatenate([pool_prev, xc], axis=1)[:, -POOL_STATE:, :]
    ob, oc, vbn = _branches(bc, halo, P["g_vb"], w["w_sp"][:, :csz, :csz], w["b_sp_full"][:csz],
                            w["w_pool"], P["pool_scale"], tmb, csz, L, pos0)

    tm2 = _pick(T, (MERGE_TM,))
    h, hp, route, route_t, counts = _merge(x, oa, ob, oc, gt, w["w_branch"], w["w_out"], P["norm2"],
                                           w["wr_hi"], w["wr_lo"], P["b_r"], w["tri_rank"], tm2)
    y = _moe_sparse(h, hp, route, route_t, counts, w["w_e_gate"], w["w_e_up"], w["w_e_down"], l, tm2)
    return y, (kf, vf), pool_new, vbn


def kernel(x_prompt, x_sample, cache_k, cache_v, state_pool, norm1, w_in, g_q, g_k, lam, g_sub,
           g_vb, w_sp, b_sp, w_pool, pool_scale, w_branch, w_out, norm2, w_rg, b_rg, w_re, b_re,
           w_e_gate, w_e_up, w_e_down):
    depth = w_in.shape[0]
    B, S, D = x_prompt.shape
    Bs, Ls, _ = x_sample.shape
    PL = cache_k.shape[2]
    ck = jnp.transpose(cache_k, (0, 1, 3, 4, 5, 2)).reshape(depth, Bs, W_A, PL)
    cv = cache_v.reshape(depth, Bs, PL, W_A)

    rep = IN_TN // HD_A
    gidx = jnp.arange(IN_TN) // HD_A
    e_mat = (gidx[:, None] == gidx[None, :]).astype(BF16)
    tri = jnp.tril(jnp.ones((SG_CHUNK, SG_CHUNK), bool))
    weg, weu, wed = w_e_gate.astype(BF16), w_e_up.astype(BF16), w_e_down.astype(BF16)
    ridx = jnp.arange(MERGE_TM)
    tri_rank = (ridx[None, :] < ridx[:, None]).astype(BF16)

    xp = x_prompt.reshape(B * S, D)
    xs = x_sample.reshape(Bs * Ls, D)
    outs = {k: [] for k in ("pp", "ps", "gs")}
    kv_p = (jnp.zeros((depth, B, W_A, S), F32), jnp.zeros((depth, B * S, W_A), F32))
    kv_s = (jnp.zeros((depth, Bs * Ls, W_A), F32), jnp.zeros((depth, Bs * Ls, W_A), F32))
    for l in range(depth):
        w_r = jnp.zeros((D, LANES), F32)
        w_r = w_r.at[:, :N_EXPERT_GROUPS].set(w_rg[l]).at[:, ROUTER_OFF:ROUTER_OFF + N_EXPERTS].set(w_re[l])
        wr_hi = w_r.astype(BF16)
        wr_lo = (w_r - wr_hi.astype(F32)).astype(BF16)
        b_r = jnp.zeros((1, LANES), F32)
        b_r = b_r.at[0, :N_EXPERT_GROUPS].set(b_rg[l]).at[0, ROUTER_OFF:ROUTER_OFF + N_EXPERTS].set(b_re[l])
        score_bound = HD_A * QK_SCALE * jnp.max(jnp.abs(g_q[l])) * jnp.max(jnp.abs(g_k[l]))
        attn_fast = (score_bound <= MAX_UNSHIFTED_SCORE).astype(jnp.int32).reshape(1)
        P = dict(norm1=norm1[l][None], norm2=norm2[l][None], attn_fast=attn_fast,
                 gq=(jnp.tile(g_q[l], rep) * QK_SCALE)[None], gk=jnp.tile(g_k[l], rep)[None],
                 gk_rows=jnp.broadcast_to(jnp.tile(g_k[l], rep)[:, None], (IN_TN, LANES)),
                 lam=lam[l], g_sub=g_sub[l][None], g_vb=g_vb[l][None],
                 pool_scale=pool_scale[l][None], b_r=b_r)
        w = dict(w_in=w_in[l, :, :GATE_COL0].astype(BF16), w_gates=w_in[l, :, GATE_COL0:].astype(BF16),
                 w_kt=jnp.transpose(w_in[l, :, W_A:2 * W_A]).astype(BF16), e_mat=e_mat,
                 w_sp=jnp.where(tri, w_sp[l], 0.0).astype(BF16),
                 b_sp_full=jnp.repeat(jnp.transpose(b_sp[l]), LANES, axis=1),
                 w_pool=w_pool[l].astype(BF16), w_branch=w_branch[l].astype(BF16),
                 w_out=w_out[l].astype(BF16), wr_hi=wr_hi, wr_lo=wr_lo,
                 w_e_gate=weg, w_e_up=weu, w_e_down=wed, tri_rank=tri_rank)
        xp, kv_p, pool1, _ = _layer(xp, l, depth, P, w, kv_p, None, None, None, B, S, True)
        xs, kv_s, pool2, vb2 = _layer(xs, l, depth, P, w, kv_s, ck, cv, state_pool[l], Bs, Ls, False)
        outs["pp"].append(pool1)
        outs["ps"].append(pool2)
        outs["gs"].append(vb2.reshape(Bs, Ls, W_B))
    st = lambda k: jnp.stack(outs[k])
    return (xp.reshape(B, S, D), xs.reshape(Bs, Ls, D),
            jnp.transpose(kv_p[0].reshape(depth, B, N_HEADS_A, 2, HD_A, S), (0, 1, 5, 2, 3, 4)),
            kv_p[1].reshape(depth, B, S, N_HEADS_A, 2 * HD_A), st("pp"),
            kv_s[0].reshape(depth, Bs, Ls, N_HEADS_A, 2, HD_A),
            kv_s[1].reshape(depth, Bs, Ls, N_HEADS_A, 2 * HD_A), st("ps"), st("gs"))
```
